```python
import math
import jax, jax.numpy as jnp
from jax import lax
import numpy as np

D_MODEL = 1024
BATCH = 4
SEQ = 4096
DEPTH = 1
DEC_BATCH = 2
DEC_SEQ = 8192
PAST_LEN = 128

PLE_DIM = 256
SSD_WIDTH = 1024
SSD_HEAD_DIM = 64
SSD_HEADS = SSD_WIDTH // SSD_HEAD_DIM
SSD_GROUPS = 2
SSD_STATE = 128
SSD_CONV = 5
SSD_CHUNK = 128
XBC_WIDTH = SSD_WIDTH + 2 * SSD_GROUPS * SSD_STATE
HY_WIDTH = 1024
HY_GROUPS = 16
HY_SHORT = 3
HY_EMB = 33
HY_FILTER_HIDDEN = 64
HY_FAST_DECAY = 0.3
HY_SLOW_DECAY = 1.5
HY_DECAY_TARGET = 1e-2
MIX_WIDTH = SSD_WIDTH + HY_WIDTH
IN_COLS = SSD_WIDTH + XBC_WIDTH + 2 * SSD_HEADS + 3 * HY_WIDTH
FFN_HIDDEN = -(-8 * D_MODEL // (3 * 256)) * 256
EPS = 1e-6

kernel_name = "hymba_ssd_hyena_bidir_encoder"


def group_rmsnorm(x, w, groups=1):
    shp = x.shape
    xf = x.astype(jnp.float32).reshape(shp[:-1] + (groups, shp[-1] // groups))
    xf = xf * lax.rsqrt(jnp.mean(xf * xf, axis=-1, keepdims=True) + EPS)
    return (xf.reshape(shp) * w.astype(jnp.float32)).astype(x.dtype)


def depthwise_conv(x, w, b):
    width = w.shape[0]
    y = lax.conv_general_dilated(
        x, w[:, None, :].astype(x.dtype), window_strides=(1,),
        padding=[(width // 2, width // 2)],
        dimension_numbers=('NWC', 'WIO', 'NWC'),
        feature_group_count=x.shape[-1])
    return y + b.astype(x.dtype)


def ssd_chunked(x, dt, a_h, bm, cm):
    b, L, H, P = x.shape
    G, N = bm.shape[2], bm.shape[3]
    R = H // G
    T = SSD_CHUNK
    c = L // T
    xr = (x * dt[..., None]).reshape(b, c, T, G, R, P)
    a = (dt * a_h).reshape(b, c, T, G, R)
    a_cs = jnp.cumsum(a, axis=2)
    br = bm.reshape(b, c, T, G, N)
    cr = cm.reshape(b, c, T, G, N)
    a_t = jnp.moveaxis(a_cs, 2, -1)
    seg = a_t[..., :, None] - a_t[..., None, :]
    causal = jnp.tril(jnp.ones((T, T), dtype=bool))
    decay_ts = jnp.exp(jnp.where(causal, seg, -jnp.inf))
    scores = jnp.einsum('bctgn,bcsgn->bcgts', cr, br)
    y_diag = jnp.einsum('bcgrts,bcsgrp->bctgrp', scores[:, :, :, None] * decay_ts, xr)
    dec_end = jnp.exp(a_cs[:, :, -1:] - a_cs)
    chunk_states = jnp.einsum('bcsgn,bcsgrp->bcgrpn', br, xr * dec_end[..., None])
    chunk_decay = jnp.exp(a_cs[:, :, -1])

    def step(carry, inp):
        s_c, d_c = inp
        return carry * d_c[..., None, None] + s_c, carry

    init = jnp.zeros((b, G, R, P, N), jnp.float32)
    _, prev = lax.scan(step, init, (jnp.moveaxis(chunk_states, 1, 0), jnp.moveaxis(chunk_decay, 1, 0)))
    prev = jnp.moveaxis(prev, 0, 1)
    y_off = jnp.einsum('bctgn,bcgrpn->bctgrp', cr, prev) * jnp.exp(a_cs)[..., None]
    return (y_diag + y_off).reshape(b, L, H, P)


def hyena_filters(L, w1, b1, w2, b2, w3, b3, w4, freq):
    f32 = jnp.float32
    pos = jnp.arange(L, dtype=f32)[:, None]
    t = pos / (L - 1)
    bands = (HY_EMB - 1) // 2
    fb = jnp.linspace(1e-4, bands - 1, bands, dtype=f32)[None, :]
    ang = fb * (2.0 * math.pi * pos / L)
    z = jnp.concatenate([t, jnp.cos(ang), -jnp.sin(ang)], axis=-1)
    fr = freq.astype(f32)
    h = jnp.sin(fr * (z @ w1.astype(f32) + b1.astype(f32)))
    h = jnp.sin(fr * (h @ w2.astype(f32) + b2.astype(f32)))
    h = jnp.sin(fr * (h @ w3.astype(f32) + b3.astype(f32)))
    h = (h @ w4.astype(f32)).reshape(L, 2, HY_WIDTH)
    max_decay = math.log(HY_DECAY_TARGET) / HY_FAST_DECAY
    min_decay = math.log(HY_DECAY_TARGET) / HY_SLOW_DECAY
    deltas = jnp.linspace(min_decay, max_decay, HY_WIDTH, dtype=f32)
    window = jnp.exp(-t * jnp.abs(deltas)[None, :])
    h = h * window[:, None, :]
    return h[:, 0], h[:, 1]


def bidir_fftconv(u, h_f, h_b):
    L, C = h_f.shape
    k = jnp.concatenate([h_f, jnp.zeros((1, C), jnp.float32), h_b[:0:-1]], axis=0)
    k_hat = jnp.fft.rfft(k, n=2 * L, axis=0)
    u_hat = jnp.fft.rfft(u.astype(jnp.float32), n=2 * L, axis=1)
    return jnp.fft.irfft(u_hat * k_hat[None], n=2 * L, axis=1)[:, :L]


def encoder_layer(h, p_l, norm_mix_pre, w_in, ssd_conv_w, ssd_conv_b, ssd_dt_bias, ssd_a_log,
                  ssd_d, ssd_norm_w, hy_conv_w, hy_conv_b, hy_f_w1, hy_f_b1, hy_f_w2, hy_f_b2,
                  hy_f_w3, hy_f_b3, hy_f_w4, hy_f_freq, hy_bias, hy_norm_w, w_out, norm_mix_post,
                  norm_ffn_pre, w_gate, w_up, w_down, norm_ffn_post, ple_norm_pre, w_ple_gate,
                  w_ple_proj, ple_norm_post):
    f32 = jnp.float32
    b, L, _ = h.shape
    dtype = h.dtype
    u = group_rmsnorm(h, norm_mix_pre)
    proj = u @ w_in
    o1 = SSD_WIDTH
    o2 = o1 + XBC_WIDTH
    o3 = o2 + 2 * SSD_HEADS
    z, xbc, dt_raw, hy = proj[..., :o1], proj[..., o1:o2], proj[..., o2:o3], proj[..., o3:]

    xbc = jax.nn.silu(depthwise_conv(xbc, ssd_conv_w, ssd_conv_b)).astype(f32)
    xs = xbc[..., :SSD_WIDTH].reshape(b, L, SSD_HEADS, SSD_HEAD_DIM)
    bm = xbc[..., SSD_WIDTH:SSD_WIDTH + SSD_GROUPS * SSD_STATE].reshape(b, L, SSD_GROUPS, SSD_STATE)
    cm = xbc[..., SSD_WIDTH + SSD_GROUPS * SSD_STATE:].reshape(b, L, SSD_GROUPS, SSD_STATE)
    dt = jax.nn.softplus(dt_raw.astype(f32).reshape(b, L, 2, SSD_HEADS) + ssd_dt_bias.astype(f32))
    a_h = -jnp.exp(ssd_a_log.astype(f32))
    y_fwd = ssd_chunked(xs, dt[:, :, 0], a_h[0], bm, cm)
    y_bwd = jnp.flip(ssd_chunked(jnp.flip(xs, 1), jnp.flip(dt[:, :, 1], 1), a_h[1],
                                 jnp.flip(bm, 1), jnp.flip(cm, 1)), 1)
    y_ssd = (y_fwd + y_bwd + xs * ssd_d.astype(f32)[:, None]).reshape(b, L, SSD_WIDTH)
    y_ssd = group_rmsnorm(y_ssd * jax.nn.silu(z.astype(f32)), ssd_norm_w, SSD_GROUPS).astype(dtype)

    hy = depthwise_conv(hy, hy_conv_w, hy_conv_b)
    x0, x1, v = hy[..., :HY_WIDTH], hy[..., HY_WIDTH:2 * HY_WIDTH], hy[..., 2 * HY_WIDTH:]
    h_f, h_b = hyena_filters(L, hy_f_w1, hy_f_b1, hy_f_w2, hy_f_b2, hy_f_w3, hy_f_b3, hy_f_w4, hy_f_freq)
    g = (v * x1).astype(f32)
    conv = bidir_fftconv(g, h_f, h_b) + g * hy_bias.astype(f32)
    y_hy = group_rmsnorm(x0.astype(f32) * conv, hy_norm_w, HY_GROUPS).astype(dtype)

    mix = jnp.concatenate([y_ssd, y_hy], axis=-1) @ w_out
    h = h + group_rmsnorm(mix, norm_mix_post)

    u = group_rmsnorm(h, norm_ffn_pre)
    ff = (jax.nn.silu(u @ w_gate) * (u @ w_up)) @ w_down
    h = h + group_rmsnorm(ff, norm_ffn_post)

    gate = jax.nn.sigmoid((group_rmsnorm(h, ple_norm_pre) @ w_ple_gate).astype(f32))
    e = (p_l @ w_ple_proj).astype(f32)
    h = h + group_rmsnorm((gate * e).astype(dtype), ple_norm_post)
    return h


def run_trunk(x, p, weights):
    h = x
    for i in range(DEPTH):
        h = encoder_layer(h, p[i], *[w[i] for w in weights])
    return h


def setup_inputs(seed: int = 0) -> dict:
    key = jax.random.key(seed)
    ks = jax.random.split(key, 40)
    f32 = jnp.float32

    def nrm(k, shape, scale):
        return scale * jax.random.normal(k, shape, f32)

    def gain(k, shape):
        return 1.0 + 0.05 * jax.random.normal(k, shape, f32)

    dt0 = jnp.exp(jax.random.uniform(ks[8], (DEPTH, 2, SSD_HEADS), f32,
                                     minval=math.log(1e-3), maxval=math.log(1e-1)))
    ssd_dt_bias = dt0 + jnp.log(-jnp.expm1(-dt0))
    ssd_a_log = jnp.log(jax.random.uniform(ks[9], (DEPTH, 2, SSD_HEADS), f32, minval=1.0, maxval=16.0))
    return {
        "x_prompt": nrm(ks[0], (BATCH, SEQ, D_MODEL), 1.0),
        "x_sample": nrm(ks[1], (DEC_BATCH, DEC_SEQ, D_MODEL), 1.0),
        "p_prompt": nrm(ks[2], (DEPTH, BATCH, SEQ, PLE_DIM), 1.0),
        "p_sample": nrm(ks[3], (DEPTH, DEC_BATCH, DEC_SEQ, PLE_DIM), 1.0),
        "norm_mix_pre": gain(ks[4], (DEPTH, D_MODEL)),
        "w_in": nrm(ks[5], (DEPTH, D_MODEL, IN_COLS), D_MODEL ** -0.5),
        "ssd_conv_w": nrm(ks[6], (DEPTH, SSD_CONV, XBC_WIDTH), SSD_CONV ** -0.5),
        "ssd_conv_b": nrm(ks[7], (DEPTH, XBC_WIDTH), 0.02),
        "ssd_dt_bias": ssd_dt_bias,
        "ssd_a_log": ssd_a_log,
        "ssd_d": 1.0 + 0.1 * jax.random.normal(ks[10], (DEPTH, SSD_HEADS), f32),
        "ssd_norm_w": gain(ks[11], (DEPTH, SSD_WIDTH)),
        "hy_conv_w": nrm(ks[12], (DEPTH, HY_SHORT, 3 * HY_WIDTH), HY_SHORT ** -0.5),
        "hy_conv_b": nrm(ks[13], (DEPTH, 3 * HY_WIDTH), 0.02),
        "hy_f_w1": nrm(ks[14], (DEPTH, HY_EMB, HY_FILTER_HIDDEN), HY_EMB ** -0.5),
        "hy_f_b1": nrm(ks[15], (DEPTH, HY_FILTER_HIDDEN), 0.02),
        "hy_f_w2": nrm(ks[16], (DEPTH, HY_FILTER_HIDDEN, HY_FILTER_HIDDEN), HY_FILTER_HIDDEN ** -0.5),
        "hy_f_b2": nrm(ks[17], (DEPTH, HY_FILTER_HIDDEN), 0.02),
        "hy_f_w3": nrm(ks[18], (DEPTH, HY_FILTER_HIDDEN, HY_FILTER_HIDDEN), HY_FILTER_HIDDEN ** -0.5),
        "hy_f_b3": nrm(ks[19], (DEPTH, HY_FILTER_HIDDEN), 0.02),
        "hy_f_w4": nrm(ks[20], (DEPTH, HY_FILTER_HIDDEN, 2 * HY_WIDTH), HY_FILTER_HIDDEN ** -0.5),
        "hy_f_freq": 1.0 + 0.1 * jax.random.normal(ks[21], (DEPTH, HY_FILTER_HIDDEN), f32),
        "hy_bias": nrm(ks[22], (DEPTH, HY_WIDTH), 0.1),
        "hy_norm_w": gain(ks[23], (DEPTH, HY_WIDTH)),
        "w_out": nrm(ks[24], (DEPTH, MIX_WIDTH, D_MODEL), MIX_WIDTH ** -0.5),
        "norm_mix_post": gain(ks[25], (DEPTH, D_MODEL)),
        "norm_ffn_pre": gain(ks[26], (DEPTH, D_MODEL)),
        "w_gate": nrm(ks[27], (DEPTH, D_MODEL, FFN_HIDDEN), D_MODEL ** -0.5),
        "w_up": nrm(ks[28], (DEPTH, D_MODEL, FFN_HIDDEN), D_MODEL ** -0.5),
        "w_down": nrm(ks[29], (DEPTH, FFN_HIDDEN, D_MODEL), FFN_HIDDEN ** -0.5),
        "norm_ffn_post": gain(ks[30], (DEPTH, D_MODEL)),
        "ple_norm_pre": gain(ks[31], (DEPTH, D_MODEL)),
        "w_ple_gate": nrm(ks[32], (DEPTH, D_MODEL, D_MODEL), D_MODEL ** -0.5),
        "w_ple_proj": nrm(ks[33], (DEPTH, PLE_DIM, D_MODEL), PLE_DIM ** -0.5),
        "ple_norm_post": gain(ks[34], (DEPTH, D_MODEL)),
    }


def reference(x_prompt, x_sample, p_prompt, p_sample, norm_mix_pre, w_in, ssd_conv_w, ssd_conv_b,
              ssd_dt_bias, ssd_a_log, ssd_d, ssd_norm_w, hy_conv_w, hy_conv_b, hy_f_w1, hy_f_b1,
              hy_f_w2, hy_f_b2, hy_f_w3, hy_f_b3, hy_f_w4, hy_f_freq, hy_bias, hy_norm_w, w_out,
              norm_mix_post, norm_ffn_pre, w_gate, w_up, w_down, norm_ffn_post, ple_norm_pre,
              w_ple_gate, w_ple_proj, ple_norm_post):
    weights = (norm_mix_pre, w_in, ssd_conv_w, ssd_conv_b, ssd_dt_bias, ssd_a_log, ssd_d, ssd_norm_w,
               hy_conv_w, hy_conv_b, hy_f_w1, hy_f_b1, hy_f_w2, hy_f_b2, hy_f_w3, hy_f_b3, hy_f_w4,
               hy_f_freq, hy_bias, hy_norm_w, w_out, norm_mix_post, norm_ffn_pre, w_gate, w_up,
               w_down, norm_ffn_post, ple_norm_pre, w_ple_gate, w_ple_proj, ple_norm_post)
    y_prompt = run_trunk(x_prompt, p_prompt, weights)
    y_sample = run_trunk(x_sample, p_sample, weights)
    return (y_prompt, y_sample)
```

```python
import functools
import math

import numpy as np
import jax
import jax.numpy as jnp
from jax import lax
from jax.experimental import pallas as pl
from jax.experimental.pallas import tpu as pltpu

F32 = jnp.float32
BF16 = jnp.bfloat16
HIGHEST = lax.Precision.HIGHEST

EPS = 1e-6
LANE = 128
SUBLANE = 8
MXU_DIM = 256
VMEM_LIMIT = 56 * 1024 * 1024

SSD_HEAD_DIM = 64
SSD_HEADS = 16
SSD_GROUPS = 2
SSD_STATE = 128
SSD_CHUNK = 128
HY_GROUP = 64
HY_EMB = 33
HY_FAST_DECAY = 0.3
HY_SLOW_DECAY = 1.5
HY_DECAY_TARGET = 1e-2
DFT_N1 = 128


def _params(sem):
    return pltpu.CompilerParams(dimension_semantics=sem, vmem_limit_bytes=VMEM_LIMIT)


def _const_spec(shape):
    nd = len(shape)
    return pl.BlockSpec(shape, lambda *_: (0,) * nd, pipeline_mode=pl.Buffered(1))


def _rms(x, w):
    return x * lax.rsqrt(jnp.mean(x * x, axis=-1, keepdims=True) + EPS) * w


def _silu(x):
    return x * jax.nn.sigmoid(x)


def _softplus(x):
    return jnp.maximum(x, 0.0) + jnp.log1p(jnp.exp(-jnp.abs(x)))


def _in_proj_kernel(x_ref, nw_ref, wz_ref, wxbc_ref, why_ref, wdt_ref, wdtT_ref,
                    z_ref, xbc_ref, hy_ref, dt_ref, dtT_ref, *, col_chunk):
    u = _rms(x_ref[...], nw_ref[...])
    ub = u.astype(BF16)
    for w_ref, o_ref in ((wz_ref, z_ref), (wxbc_ref, xbc_ref), (why_ref, hy_ref)):
        for j in range(0, w_ref.shape[1], col_chunk):
            o_ref[:, j:j + col_chunk] = jnp.dot(ub, w_ref[:, j:j + col_chunk],
                                                preferred_element_type=F32)
    for d in range(2):
        dt_ref[d] = jnp.dot(u, wdt_ref[d], precision=HIGHEST, preferred_element_type=F32)
        dtT_ref[d] = lax.dot_general(wdtT_ref[d], u, (((1,), (1,)), ((), ())),
                                     precision=HIGHEST, preferred_element_type=F32)


def _in_proj(x2d, nw, wz, wxbc, why, wdt, wdtT, tm=512):
    R, D = x2d.shape
    H = wdt.shape[2]
    kern = functools.partial(_in_proj_kernel, col_chunk=512)
    return pl.pallas_call(
        kern,
        grid=(R // tm,),
        in_specs=[
            pl.BlockSpec((tm, D), lambda i: (i, 0)),
            _const_spec(nw.shape), _const_spec(wz.shape), _const_spec(wxbc.shape),
            _const_spec(why.shape), _const_spec(wdt.shape), _const_spec(wdtT.shape),
        ],
        out_specs=[
            pl.BlockSpec((tm, wz.shape[1]), lambda i: (i, 0)),
            pl.BlockSpec((tm, wxbc.shape[1]), lambda i: (i, 0)),
            pl.BlockSpec((tm, why.shape[1]), lambda i: (i, 0)),
            pl.BlockSpec((2, tm, H), lambda i: (0, i, 0)),
            pl.BlockSpec((2, H, tm), lambda i: (0, 0, i)),
        ],
        out_shape=[
            jax.ShapeDtypeStruct((R, wz.shape[1]), F32),
            jax.ShapeDtypeStruct((R, wxbc.shape[1]), F32),
            jax.ShapeDtypeStruct((R, why.shape[1]), F32),
            jax.ShapeDtypeStruct((2, R, H), F32),
            jax.ShapeDtypeStruct((2, H, R), F32),
        ],
        compiler_params=_params(("arbitrary",)),
        name="in_proj",
    )(x2d, nw, wz, wxbc, why, wdt, wdtT)


def _ssd_kernel(xm_ref, xp_ref, xn_ref, dt_ref, dtT_ref, cw_ref, cb_ref, bias_ref, biasT_ref,
                alog_ref, alogT_ref, dskip_ref, tri_ref, e_ref, y_ref, state_ref, xpad_ref,
                *, nchunks, conv_w):
    d = pl.program_id(1)
    i = pl.program_id(2)
    ch = i + d * (nchunks - 1 - 2 * i)
    T = xm_ref.shape[0]
    W = SSD_HEADS * SSD_HEAD_DIM
    GN = SSD_GROUPS * SSD_STATE
    GW = W // SSD_GROUPS
    half = conv_w // 2

    @pl.when(i == 0)
    def _():
        state_ref[...] = jnp.zeros_like(state_ref)

    xpad_ref[0:SUBLANE] = jnp.where(ch == 0, 0.0, xp_ref[...])
    xpad_ref[SUBLANE:SUBLANE + T] = xm_ref[...]
    xpad_ref[SUBLANE + T:2 * SUBLANE + T] = jnp.where(ch == nchunks - 1, 0.0, xn_ref[...])
    acc = cb_ref[...] + cw_ref[0:1, :] * xpad_ref[SUBLANE - half:SUBLANE - half + T, :]
    for j in range(1, conv_w):
        o = SUBLANE - half + j
        acc = acc + cw_ref[j:j + 1, :] * xpad_ref[o:o + T, :]
    act = _silu(acc)
    xs = act[:, :W]
    bm = act[:, W:W + GN].astype(BF16)
    cm = act[:, W + GN:].astype(BF16)

    dtv = _softplus(dt_ref[...] + bias_ref[...])
    dtT = _softplus(dtT_ref[...] + biasT_ref[...])
    a = dtv * (-jnp.exp(alog_ref[...]))
    aT = dtT * (-jnp.exp(alogT_ref[...]))
    tri = tri_ref[...]
    mask = tri > 0.5
    cs = jnp.dot(tri, a, precision=HIGHEST, preferred_element_type=F32)
    csT = lax.dot_general(aT, tri, (((1,), (1,)), ((), ())), precision=HIGHEST,
                          preferred_element_type=F32)
    tot = jnp.sum(a, axis=0, keepdims=True)

    e = e_ref[...]
    expand = lambda v: jnp.dot(v, e, precision=HIGHEST, preferred_element_type=F32)
    xdt = xs * expand(dtv)
    ecs_e = expand(jnp.exp(cs))
    xdec = (xdt * expand(jnp.exp(tot - cs))).astype(BF16)
    cdec_e = expand(jnp.broadcast_to(jnp.exp(tot), (SUBLANE, SSD_HEADS)))[0:1]
    dskip = jnp.where(d == 0, dskip_ref[...], 0.0)
    lane = lax.broadcasted_iota(jnp.int32, (T, LANE), 1)
    lo_half = lane < SSD_HEAD_DIM

    heads_per_group = SSD_HEADS // SSD_GROUPS
    for g in range(SSD_GROUPS):
        cg = cm[:, g * SSD_STATE:(g + 1) * SSD_STATE]
        bg = bm[:, g * SSD_STATE:(g + 1) * SSD_STATE]
        gl = slice(g * GW, (g + 1) * GW)
        scores = lax.dot_general(cg, bg, (((1,), (1,)), ((), ())), preferred_element_type=F32)
        st = state_ref[:, gl]
        yoff = jnp.dot(cg, st.astype(BF16), preferred_element_type=F32) * ecs_e[:, gl]
        for pr in range(heads_per_group // 2):
            h0 = g * heads_per_group + 2 * pr
            ls = slice(h0 * SSD_HEAD_DIM, h0 * SSD_HEAD_DIM + LANE)
            xpair = xdt[:, ls]
            yp = yoff[:, pr * LANE:(pr + 1) * LANE] + xs[:, ls] * dskip[:, ls]
            for k in range(2):
                h = h0 + k
                seg = cs[:, h:h + 1] - csT[h:h + 1, :]
                m = (scores * jnp.exp(jnp.where(mask, seg, -1e30))).astype(BF16)
                xk = jnp.where(lo_half if k == 0 else ~lo_half, xpair, 0.0).astype(BF16)
                yp = yp + jnp.dot(m, xk, preferred_element_type=F32)
            y_ref[:, ls] = yp
        upd = lax.dot_general(bg, xdec[:, gl], (((0,), (0,)), ((), ())), preferred_element_type=F32)
        state_ref[:, gl] = st * cdec_e[:, gl] + upd


def _ssd(xbc, dt, dtT, conv_w, conv_b, dt_bias, a_log, ssd_d, batch, seqlen):
    R, XW = xbc.shape
    T = SSD_CHUNK
    c = seqlen // T
    W = SSD_HEADS * SSD_HEAD_DIM
    width = conv_w.shape[0]
    cw = jnp.zeros((SUBLANE, XW), F32).at[:width].set(conv_w)
    cb = conv_b.reshape(1, XW)
    bias = dt_bias.reshape(2, 1, SSD_HEADS)
    biasT = dt_bias.reshape(2, SSD_HEADS, 1)
    alog = a_log.reshape(2, 1, SSD_HEADS)
    alogT = a_log.reshape(2, SSD_HEADS, 1)
    dskip = jnp.repeat(ssd_d, SSD_HEAD_DIM).reshape(1, W)
    lower = np.tril(np.ones((T, T), np.float32))
    tri = jnp.asarray(np.stack([lower, lower.T]))
    e = jnp.asarray(np.repeat(np.eye(SSD_HEADS, dtype=np.float32), SSD_HEAD_DIM, axis=1))
    rows8 = R // SUBLANE
    tb = T // SUBLANE

    def ch_of(d, i):
        return i + d * (c - 1 - 2 * i)

    def blk(b, d, i):
        return b * c + ch_of(d, i)

    kern = functools.partial(_ssd_kernel, nchunks=c, conv_w=width)
    return pl.pallas_call(
        kern,
        grid=(batch, 2, c),
        in_specs=[
            pl.BlockSpec((T, XW), lambda b, d, i: (blk(b, d, i), 0)),
            pl.BlockSpec((SUBLANE, XW), lambda b, d, i: (jnp.maximum(blk(b, d, i) * tb - 1, 0), 0)),
            pl.BlockSpec((SUBLANE, XW), lambda b, d, i: (jnp.minimum((blk(b, d, i) + 1) * tb, rows8 - 1), 0)),
            pl.BlockSpec((None, T, SSD_HEADS), lambda b, d, i: (d, blk(b, d, i), 0)),
            pl.BlockSpec((None, SSD_HEADS, T), lambda b, d, i: (d, 0, blk(b, d, i))),
            _const_spec(cw.shape), _const_spec(cb.shape),
            pl.BlockSpec((None, 1, SSD_HEADS), lambda b, d, i: (d, 0, 0)),
            pl.BlockSpec((None, SSD_HEADS, 1), lambda b, d, i: (d, 0, 0)),
            pl.BlockSpec((None, 1, SSD_HEADS), lambda b, d, i: (d, 0, 0)),
            pl.BlockSpec((None, SSD_HEADS, 1), lambda b, d, i: (d, 0, 0)),
            _const_spec(dskip.shape),
            pl.BlockSpec((None, T, T), lambda b, d, i: (d, 0, 0)),
            _const_spec(e.shape),
        ],
        out_specs=pl.BlockSpec((None, T, W), lambda b, d, i: (d, blk(b, d, i), 0)),
        out_shape=jax.ShapeDtypeStruct((2, R, W), F32),
        scratch_shapes=[
            pltpu.VMEM((SSD_STATE, W), F32),
            pltpu.VMEM((T + 2 * SUBLANE, XW), F32),
        ],
        compiler_params=_params(("arbitrary", "arbitrary", "arbitrary")),
        name="ssd",
    )(xbc, xbc, xbc, dt, dtT, cw, cb, bias, biasT, alog, alogT, dskip, tri, e)


def _hy_conv_kernel(xm_ref, xp_ref, xn_ref, cw_ref, cb_ref, g_ref, x0_ref, xpad_ref, *, ntiles, conv_w):
    i = pl.program_id(1)
    TL = xm_ref.shape[0]
    C = g_ref.shape[1]
    half = conv_w // 2
    xpad_ref[0:SUBLANE] = jnp.where(i == 0, 0.0, xp_ref[...])
    xpad_ref[SUBLANE:SUBLANE + TL] = xm_ref[...]
    xpad_ref[SUBLANE + TL:2 * SUBLANE + TL] = jnp.where(i == ntiles - 1, 0.0, xn_ref[...])

    def conv(cols):
        acc = cb_ref[:, cols]
        for j in range(conv_w):
            o = SUBLANE - half + j
            acc = acc + cw_ref[j:j + 1, cols] * xpad_ref[o:o + TL, cols]
        return acc

    x0_ref[...] = conv(slice(0, C))
    g_ref[...] = conv(slice(2 * C, 3 * C)) * conv(slice(C, 2 * C))


def _hy_conv(hy, conv_w, conv_b, batch, seqlen, tl=256):
    R, HW = hy.shape
    C = HW // 3
    width = conv_w.shape[0]
    cw = jnp.zeros((SUBLANE, HW), F32).at[:width].set(conv_w)
    cb = conv_b.reshape(1, HW)
    nt = seqlen // tl
    tb = tl // SUBLANE
    rows8 = R // SUBLANE
    blk = lambda b, i: b * nt + i
    kern = functools.partial(_hy_conv_kernel, ntiles=nt, conv_w=width)
    return pl.pallas_call(
        kern,
        grid=(batch, nt),
        in_specs=[
            pl.BlockSpec((tl, HW), lambda b, i: (blk(b, i), 0)),
            pl.BlockSpec((SUBLANE, HW), lambda b, i: (jnp.maximum(blk(b, i) * tb - 1, 0), 0)),
            pl.BlockSpec((SUBLANE, HW), lambda b, i: (jnp.minimum((blk(b, i) + 1) * tb, rows8 - 1), 0)),
            _const_spec(cw.shape), _const_spec(cb.shape),
        ],
        out_specs=[
            pl.BlockSpec((tl, C), lambda b, i: (blk(b, i), 0)),
            pl.BlockSpec((tl, C), lambda b, i: (blk(b, i), 0)),
        ],
        out_shape=[jax.ShapeDtypeStruct((R, C), F32), jax.ShapeDtypeStruct((R, C), F32)],
        scratch_shapes=[pltpu.VMEM((tl + 2 * SUBLANE, HW), F32)],
        compiler_params=_params(("arbitrary", "arbitrary")),
        name="hy_conv",
    )(hy, hy, hy, cw, cb)


def _filt_gen_kernel(fb_ref, w1_ref, b1_ref, w2_ref, b2_ref, w3_ref, b3_ref, w4_ref, fr_ref, dl_ref,
                     k_ref, *, seqlen):
    TF = k_ref.shape[0]
    L = seqlen
    n = pl.program_id(0) * TF + lax.broadcasted_iota(jnp.int32, (TF, 1), 0)
    pos = jnp.where(n <= L, n, 2 * L - n).astype(F32)
    t = pos / (L - 1)
    lane = lax.broadcasted_iota(jnp.int32, (1, LANE), 1)
    bands = (HY_EMB - 1) // 2
    ang = fb_ref[...] * (2.0 * math.pi * pos / L)
    z = jnp.where(lane == 0, t,
                  jnp.where(lane <= bands, jnp.cos(ang),
                            jnp.where(lane <= 2 * bands, -jnp.sin(ang), 0.0)))
    fr = fr_ref[...]
    dot = lambda x, w_ref: jnp.dot(x, w_ref[...], precision=HIGHEST, preferred_element_type=F32)
    h = jnp.sin(fr * (dot(z, w1_ref) + b1_ref[...]))
    h = jnp.sin(fr * (dot(h, w2_ref) + b2_ref[...]))
    h = jnp.sin(fr * (dot(h, w3_ref) + b3_ref[...]))
    h = dot(h, w4_ref) * jnp.exp(-t * dl_ref[...])
    k_ref[...] = jnp.where(n == L, 0.0, h)


def _filt_gen(seqlen, w1, b1, w2, b2, w3, b3, w4, freq, tf=512):
    L = seqlen
    hidden = w1.shape[1]
    C = w4.shape[1] // 2
    bands = (HY_EMB - 1) // 2
    fb = np.zeros((1, LANE), np.float64)
    fbv = np.linspace(1e-4, bands - 1, bands)
    fb[0, 1:1 + bands] = fbv
    fb[0, 1 + bands:1 + 2 * bands] = fbv
    max_decay = math.log(HY_DECAY_TARGET) / HY_FAST_DECAY
    min_decay = math.log(HY_DECAY_TARGET) / HY_SLOW_DECAY
    dl = np.abs(np.linspace(min_decay, max_decay, C)).reshape(1, C)
    w1p = jnp.zeros((LANE, hidden), F32).at[:HY_EMB].set(w1)
    nhalf = L // tf
    return pl.pallas_call(
        functools.partial(_filt_gen_kernel, seqlen=L),
        grid=(2 * nhalf,),
        in_specs=[
            _const_spec((1, LANE)), _const_spec(w1p.shape), _const_spec((1, hidden)),
            _const_spec(w2.shape), _const_spec((1, hidden)),
            _const_spec(w3.shape), _const_spec((1, hidden)),
            pl.BlockSpec((hidden, C), lambda i: (0, i // nhalf)),
            _const_spec((1, hidden)), _const_spec((1, C)),
        ],
        out_specs=pl.BlockSpec((tf, C), lambda i: (i, 0)),
        out_shape=jax.ShapeDtypeStruct((2 * L, C), F32),
        compiler_params=_params(("arbitrary",)),
        name="filt_gen",
    )(jnp.asarray(fb, F32), w1p, b1.reshape(1, hidden), w2, b2.reshape(1, hidden), w3,
      b3.reshape(1, hidden), w4, freq.reshape(1, hidden), jnp.asarray(dl, F32))


@functools.lru_cache(maxsize=None)
def _dft_tables(seqlen):
    N = 2 * seqlen
    N1 = DFT_N1
    N2 = N // N1
    K = N1 // 2
    k1 = np.arange(N1).reshape(1, N1, 1)
    n1 = np.arange(N1).reshape(1, 1, N1)
    n2 = np.arange(N2).reshape(N2, 1, 1)
    ph = (n2 * k1 + N2 * n1 * k1) % N
    ang = -2.0 * np.pi * ph / N
    gr, gi = np.cos(ang), np.sin(ang)
    gk = np.concatenate([gr, gi], axis=1)
    grh, gih = gr[:, :, :K], gi[:, :, :K]
    ga = np.concatenate([np.concatenate([grh, -gih], axis=2),
                         np.concatenate([gih, grh], axis=2)], axis=1)
    grt, git = np.swapaxes(grh, 1, 2), np.swapaxes(gih, 1, 2)
    gc = np.concatenate([np.concatenate([grt, git], axis=2),
                         np.concatenate([-git, grt], axis=2)], axis=1)
    k2 = np.arange(N2).reshape(N2, 1)
    m2 = np.arange(N2).reshape(1, N2)
    ang2 = -2.0 * np.pi * ((k2 * m2) % N2) / N2
    fr, fi = np.cos(ang2), np.sin(ang2)
    fb = np.block([[fr, -fi], [fi, fr]])
    fbc = np.block([[fr, fi], [-fi, fr]])
    to = lambda x: x.astype(np.float32)
    return to(gk), to(ga), to(gc), to(fb), to(fbc)


def _mxu_tables(seqlen):
    return tuple(jnp.asarray(t).astype(BF16) for t in _dft_tables(seqlen))


def _stack_bf16(re, im):
    return jnp.concatenate([re, im], axis=0).astype(BF16)


def _filt_a_kernel(k_ref, g_ref, o_ref, *, nb, C, cc):
    N1 = k_ref.shape[0]
    for j in range(nb):
        for c0 in range(0, C, cc):
            cols = slice(j * C + c0, j * C + c0 + cc)
            y = jnp.dot(g_ref[j], k_ref[:, cols].astype(BF16), preferred_element_type=F32)
            o_ref[0, :, cols] = y[:N1]
            o_ref[1, :, cols] = y[N1:]


def _filt_b_kernel(a_ref, f_ref, o_ref, *, kb, C, cc, scale):
    N2 = a_ref.shape[2]
    for q in range(kb):
        for c0 in range(0, C, cc):
            cols = slice(c0, c0 + cc)
            z = jnp.dot(f_ref[...], _stack_bf16(a_ref[0, q, :, cols], a_ref[1, q, :, cols]),
                        preferred_element_type=F32) * scale
            o_ref[0, q, :, cols] = z[:N2]
            o_ref[1, q, :, cols] = z[N2:]


def _filter_spectrum(k, seqlen, nb, kb, cc=512):
    N, C = k.shape
    N1 = DFT_N1
    N2 = N // N1
    gk, _, _, fb, _ = _mxu_tables(seqlen)
    ka = pl.pallas_call(
        functools.partial(_filt_a_kernel, nb=nb, C=C, cc=cc),
        grid=(N2 // nb,),
        in_specs=[pl.BlockSpec((N1, nb * C), lambda j: (0, j)),
                  pl.BlockSpec((nb, 2 * N1, N1), lambda j: (j, 0, 0))],
        out_specs=pl.BlockSpec((2, N1, nb * C), lambda j: (0, 0, j)),
        out_shape=jax.ShapeDtypeStruct((2, N1, N2 * C), F32),
        compiler_params=_params(("arbitrary",)),
        name="filt_a",
    )(k.reshape(N1, N2 * C), gk)
    kh = pl.pallas_call(
        functools.partial(_filt_b_kernel, kb=kb, C=C, cc=cc, scale=1.0 / N),
        grid=(N1 // kb,),
        in_specs=[pl.BlockSpec((2, kb, N2, C), lambda j: (0, j, 0, 0)), _const_spec(fb.shape)],
        out_specs=pl.BlockSpec((2, kb, N2, C), lambda j: (0, j, 0, 0)),
        out_shape=jax.ShapeDtypeStruct((2, N1, N2, C), F32),
        compiler_params=_params(("arbitrary",)),
        name="filt_b",
    )(ka.reshape(2, N1, N2, C), fb)
    return kh


def _fft_a_kernel(x_ref, g_ref, o_ref, *, nb, C, cc):
    N1 = o_ref.shape[1]
    for j in range(nb):
        for c0 in range(0, C, cc):
            cols = slice(j * C + c0, j * C + c0 + cc)
            y = jnp.dot(g_ref[j], _stack_bf16(x_ref[0, :, cols], x_ref[1, :, cols]),
                        preferred_element_type=F32)
            o_ref[0, :, cols] = y[:N1]
            o_ref[1, :, cols] = y[N1:]


def _fft_b_kernel(a_ref, kh_ref, f_ref, fc_ref, o_ref, *, kb, C, cc):
    N2 = a_ref.shape[2]
    for q in range(kb):
        for c0 in range(0, C, cc):
            cols = slice(c0, c0 + cc)
            z = jnp.dot(f_ref[...], _stack_bf16(a_ref[0, q, :, cols], a_ref[1, q, :, cols]),
                        preferred_element_type=F32)
            zr, zi = z[:N2], z[N2:]
            kr, ki = kh_ref[0, q, :, cols], kh_ref[1, q, :, cols]
            v = jnp.dot(fc_ref[...], _stack_bf16(zr * kr - zi * ki, zr * ki + zi * kr),
                        preferred_element_type=F32)
            o_ref[0, q, :, cols] = v[:N2]
            o_ref[1, q, :, cols] = v[N2:]


def _fft_c_kernel(bt_ref, gc_ref, g_ref, x0_ref, bias_ref, nw_ref, avg_ref, o_ref, *, nb, C, cc):
    K = o_ref.shape[1]
    for j in range(nb):
        for c0 in range(0, C, cc):
            cols = slice(j * C + c0, j * C + c0 + cc)
            pc = slice(c0, c0 + cc)
            y = jnp.dot(gc_ref[j], _stack_bf16(bt_ref[0, :, cols], bt_ref[1, :, cols]),
                        preferred_element_type=F32)
            for s in range(2):
                conv = y[s * K:(s + 1) * K]
                v = x0_ref[s, :, cols] * (conv + g_ref[s, :, cols] * bias_ref[:, pc])
                sq = v * v
                hi = sq.astype(BF16)
                lo = (sq - hi.astype(F32)).astype(BF16)
                ms = jnp.concatenate(
                    [jnp.dot(hi[:, m:m + MXU_DIM], avg_ref[...], preferred_element_type=F32)
                     + jnp.dot(lo[:, m:m + MXU_DIM], avg_ref[...], preferred_element_type=F32)
                     for m in range(0, cc, MXU_DIM)], axis=1)
                o_ref[s, :, cols] = v * lax.rsqrt(ms + EPS) * nw_ref[:, pc]


def _long_conv(g, x0, kh, hy_bias, hy_norm_w, batch, seqlen, nb, kb, cc=512):
    R, C = g.shape
    L = seqlen
    N = 2 * L
    N1 = DFT_N1
    N2 = N // N1
    K = N1 // 2
    bp = batch // 2
    _, ga, gc, fb, fbc = _mxu_tables(L)
    gv = g.reshape(bp, 2, K, N2 * C)
    x0v = x0.reshape(bp, 2, K, N2 * C)
    a = pl.pallas_call(
        functools.partial(_fft_a_kernel, nb=nb, C=C, cc=cc),
        grid=(bp, N2 // nb),
        in_specs=[pl.BlockSpec((None, 2, K, nb * C), lambda p, j: (p, 0, 0, j)),
                  pl.BlockSpec((nb, 2 * N1, N1), lambda p, j: (j, 0, 0))],
        out_specs=pl.BlockSpec((None, 2, N1, nb * C), lambda p, j: (p, 0, 0, j)),
        out_shape=jax.ShapeDtypeStruct((bp, 2, N1, N2 * C), F32),
        compiler_params=_params(("arbitrary", "arbitrary")),
        name="fft_a",
    )(gv, ga)
    bt = pl.pallas_call(
        functools.partial(_fft_b_kernel, kb=kb, C=C, cc=cc),
        grid=(N1 // kb, bp),
        in_specs=[pl.BlockSpec((None, 2, kb, N2, C), lambda j, p: (p, 0, j, 0, 0)),
                  pl.BlockSpec((2, kb, N2, C), lambda j, p: (0, j, 0, 0)),
                  _const_spec(fb.shape), _const_spec(fbc.shape)],
        out_specs=pl.BlockSpec((None, 2, kb, N2, C), lambda j, p: (p, 0, j, 0, 0)),
        out_shape=jax.ShapeDtypeStruct((bp, 2, N1, N2, C), F32),
        compiler_params=_params(("arbitrary", "arbitrary")),
        name="fft_b",
    )(a.reshape(bp, 2, N1, N2, C), kh, fb, fbc)
    avg = np.kron(np.eye(MXU_DIM // HY_GROUP), np.full((HY_GROUP, HY_GROUP), 1.0 / HY_GROUP))
    y = pl.pallas_call(
        functools.partial(_fft_c_kernel, nb=nb, C=C, cc=cc),
        grid=(bp, N2 // nb),
        in_specs=[pl.BlockSpec((None, 2, N1, nb * C), lambda p, j: (p, 0, 0, j)),
                  pl.BlockSpec((nb, N1, 2 * N1), lambda p, j: (j, 0, 0)),
                  pl.BlockSpec((None, 2, K, nb * C), lambda p, j: (p, 0, 0, j)),
                  pl.BlockSpec((None, 2, K, nb * C), lambda p, j: (p, 0, 0, j)),
                  _const_spec((1, C)), _const_spec((1, C)), _const_spec((MXU_DIM, MXU_DIM))],
        out_specs=pl.BlockSpec((None, 2, K, nb * C), lambda p, j: (p, 0, 0, j)),
        out_shape=jax.ShapeDtypeStruct((bp, 2, K, N2 * C), F32),
        compiler_params=_params(("arbitrary", "arbitrary")),
        name="fft_c",
    )(bt.reshape(bp, 2, N1, N2 * C), gc, gv, x0v, hy_bias.reshape(1, C), hy_norm_w.reshape(1, C),
      jnp.asarray(avg.astype(np.float32)).astype(BF16))
    return y.reshape(R, C)


def _mix_kernel(yf_ref, yb_ref, z_ref, yhy_ref, h_ref, p_ref, snw_ref, wos_ref, woh_ref, nmp_ref,
                nfp_ref, wg_ref, wu_ref, wd_ref, nfo_ref, npp_ref, wpg_ref, wpp_ref, npo_ref, o_ref):
    dot = lambda x, w_ref: jnp.dot(x.astype(BF16), w_ref[...], preferred_element_type=F32)
    ys = (yf_ref[...] + yb_ref[...]) * _silu(z_ref[...])
    gw = ys.shape[1] // SSD_GROUPS
    ysn = jnp.concatenate(
        [_rms(ys[:, k * gw:(k + 1) * gw], snw_ref[:, k * gw:(k + 1) * gw]) for k in range(SSD_GROUPS)],
        axis=1)
    mix = dot(ysn, wos_ref) + dot(yhy_ref[...], woh_ref)
    h1 = h_ref[...] + _rms(mix, nmp_ref[...])
    u = _rms(h1, nfp_ref[...])
    ff = dot(_silu(dot(u, wg_ref)) * dot(u, wu_ref), wd_ref)
    h2 = h1 + _rms(ff, nfo_ref[...])
    gate = jax.nn.sigmoid(dot(_rms(h2, npp_ref[...]), wpg_ref))
    emb = dot(p_ref[...], wpp_ref)
    o_ref[...] = h2 + _rms(gate * emb, npo_ref[...])


def _mix(y2, z, yhy, h, p, consts, tm=256):
    R, D = h.shape
    row = lambda w: pl.BlockSpec((tm, w), lambda i: (i, 0))
    return pl.pallas_call(
        _mix_kernel,
        grid=(R // tm,),
        in_specs=[
            pl.BlockSpec((None, tm, D), lambda i: (0, i, 0)),
            pl.BlockSpec((None, tm, D), lambda i: (1, i, 0)),
            row(D), row(D), row(D), row(p.shape[1]),
        ] + [_const_spec(c.shape) for c in consts],
        out_specs=row(D),
        out_shape=jax.ShapeDtypeStruct((R, D), F32),
        compiler_params=_params(("arbitrary",)),
        name="mix",
    )(y2, y2, z, yhy, h, p, *consts)


def _layer(x, p, w):
    batch, seqlen, D = x.shape
    R = batch * seqlen
    x2d = x.reshape(R, D)
    row = lambda v: v.reshape(1, -1)
    o1 = SSD_HEADS * SSD_HEAD_DIM
    o2 = o1 + w["ssd_conv_w"].shape[1]
    o3 = o2 + 2 * SSD_HEADS
    w_in = w["w_in"]
    wdt = w_in[:, o2:o3].reshape(D, 2, SSD_HEADS).transpose(1, 0, 2)
    z, xbc, hy, dt, dtT = _in_proj(
        x2d, row(w["norm_mix_pre"]), w_in[:, :o1].astype(BF16), w_in[:, o1:o2].astype(BF16),
        w_in[:, o3:].astype(BF16), wdt, wdt.transpose(0, 2, 1))

    y2 = _ssd(xbc, dt, dtT, w["ssd_conv_w"], w["ssd_conv_b"], w["ssd_dt_bias"], w["ssd_a_log"],
              w["ssd_d"], batch, seqlen)

    g, x0 = _hy_conv(hy, w["hy_conv_w"], w["hy_conv_b"], batch, seqlen)
    n2 = 2 * seqlen // DFT_N1
    nb = min(4, n2)
    kb = max(1, min(DFT_N1, 256 // n2))
    k = _filt_gen(seqlen, w["hy_f_w1"], w["hy_f_b1"], w["hy_f_w2"], w["hy_f_b2"], w["hy_f_w3"],
                  w["hy_f_b3"], w["hy_f_w4"], w["hy_f_freq"], tf=min(512, seqlen))
    kh = _filter_spectrum(k, seqlen, nb, kb)
    yhy = _long_conv(g, x0, kh, w["hy_bias"], w["hy_norm_w"], batch, seqlen, nb, kb)

    w_out = w["w_out"].astype(BF16)
    consts = [row(w["ssd_norm_w"]), w_out[:o1], w_out[o1:], row(w["norm_mix_post"]),
              row(w["norm_ffn_pre"]), w["w_gate"].astype(BF16), w["w_up"].astype(BF16),
              w["w_down"].astype(BF16), row(w["norm_ffn_post"]), row(w["ple_norm_pre"]),
              w["w_ple_gate"].astype(BF16), w["w_ple_proj"].astype(BF16), row(w["ple_norm_post"])]
    out = _mix(y2, z, yhy, x2d, p.reshape(R, -1), consts)
    return out.reshape(batch, seqlen, D)


_WEIGHT_NAMES = (
    "norm_mix_pre", "w_in", "ssd_conv_w", "ssd_conv_b", "ssd_dt_bias", "ssd_a_log", "ssd_d",
    "ssd_norm_w", "hy_conv_w", "hy_conv_b", "hy_f_w1", "hy_f_b1", "hy_f_w2", "hy_f_b2", "hy_f_w3",
    "hy_f_b3", "hy_f_w4", "hy_f_freq", "hy_bias", "hy_norm_w", "w_out", "norm_mix_post",
    "norm_ffn_pre", "w_gate", "w_up", "w_down", "norm_ffn_post", "ple_norm_pre", "w_ple_gate",
    "w_ple_proj", "ple_norm_post")


def _trunk(x, p, weights):
    h = x
    for i in range(p.shape[0]):
        h = _layer(h, p[i], {n: v[i] for n, v in zip(_WEIGHT_NAMES, weights)})
    return h


def kernel(x_prompt, x_sample, p_prompt, p_sample, norm_mix_pre, w_in, ssd_conv_w, ssd_conv_b, ssd_dt_bias, ssd_a_log, ssd_d, ssd_norm_w, hy_conv_w, hy_conv_b, hy_f_w1, hy_f_b1, hy_f_w2, hy_f_b2, hy_f_w3, hy_f_b3, hy_f_w4, hy_f_freq, hy_bias, hy_norm_w, w_out, norm_mix_post, norm_ffn_pre, w_gate, w_up, w_down, norm_ffn_post, ple_norm_pre, w_ple_gate, w_ple_proj, ple_norm_post):
    weights = (norm_mix_pre, w_in, ssd_conv_w, ssd_conv_b, ssd_dt_bias, ssd_a_log, ssd_d, ssd_norm_w,
               hy_conv_w, hy_conv_b, hy_f_w1, hy_f_b1, hy_f_w2, hy_f_b2, hy_f_w3, hy_f_b3, hy_f_w4,
               hy_f_freq, hy_bias, hy_norm_w, w_out, norm_mix_post, norm_ffn_pre, w_gate, w_up,
               w_down, norm_ffn_post, ple_norm_pre, w_ple_gate, w_ple_proj, ple_norm_post)
    return (_trunk(x_prompt, p_prompt, weights), _trunk(x_sample, p_sample, weights))
```

```python
import functools
import math

import numpy as np
import jax
import jax.numpy as jnp
from jax import lax
from jax.experimental import pallas as pl
from jax.experimental.pallas import tpu as pltpu

F32 = jnp.float32
BF16 = jnp.bfloat16
HIGHEST = lax.Precision.HIGHEST

EPS = 1e-6
LANE = 128
SUBLANE = 8
MXU_DIM = 256
VMEM_LIMIT = 56 * 1024 * 1024

SSD_HEAD_DIM = 64
SSD_HEADS = 16
SSD_GROUPS = 2
SSD_STATE = 128
SSD_CHUNK = 128
HY_GROUP = 64
HY_EMB = 33
HY_FAST_DECAY = 0.3
HY_SLOW_DECAY = 1.5
HY_DECAY_TARGET = 1e-2
DFT_N1 = 128


def _params(sem):
    return pltpu.CompilerParams(dimension_semantics=sem, vmem_limit_bytes=VMEM_LIMIT)


def _const_spec(shape):
    nd = len(shape)
    return pl.BlockSpec(shape, lambda *_: (0,) * nd, pipeline_mode=pl.Buffered(1))


def _rms(x, w):
    return x * lax.rsqrt(jnp.mean(x * x, axis=-1, keepdims=True) + EPS) * w


def _silu(x):
    return x * jax.nn.sigmoid(x)


def _softplus(x):
    return jnp.maximum(x, 0.0) + jnp.log1p(jnp.exp(-jnp.abs(x)))


def _split_bf16(x):
    hi = x.astype(BF16)
    return hi, (x - hi.astype(F32)).astype(BF16)


def _dot3(x, w_hi, w_lo):
    x_hi, x_lo = _split_bf16(x)
    dot = lambda a, b: jnp.dot(a, b, preferred_element_type=F32)
    return dot(x_hi, w_hi) + (dot(x_hi, w_lo) + dot(x_lo, w_hi))


def _in_proj_kernel(x_ref, nw_ref, wz_ref, wxbc_ref, why_ref, wdth_ref, wdtl_ref,
                    z_ref, xbc_ref, hy_ref, dt_ref, *, col_chunk):
    u = _rms(x_ref[...], nw_ref[...])
    ub, ul = _split_bf16(u)
    for w_ref, o_ref in ((wz_ref, z_ref), (wxbc_ref, xbc_ref), (why_ref, hy_ref)):
        for j in range(0, w_ref.shape[1], col_chunk):
            o_ref[:, j:j + col_chunk] = jnp.dot(ub, w_ref[:, j:j + col_chunk],
                                                preferred_element_type=F32)
    dot = lambda a, b_ref: jnp.dot(a, b_ref[...], preferred_element_type=F32)
    dt_ref[...] = dot(ub, wdth_ref) + (dot(ub, wdtl_ref) + dot(ul, wdth_ref))


def _in_proj(x2d, nw, wz, wxbc, why, wdt, tm=512):
    R, D = x2d.shape
    wdth, wdtl = _split_bf16(wdt)
    kern = functools.partial(_in_proj_kernel, col_chunk=512)
    widths = (wz.shape[1], wxbc.shape[1], why.shape[1], wdt.shape[1])
    return pl.pallas_call(
        kern,
        grid=(R // tm,),
        in_specs=[pl.BlockSpec((tm, D), lambda i: (i, 0))]
        + [_const_spec(a.shape) for a in (nw, wz, wxbc, why, wdth, wdtl)],
        out_specs=[pl.BlockSpec((tm, w), lambda i: (i, 0)) for w in widths],
        out_shape=[jax.ShapeDtypeStruct((R, w), F32) for w in widths],
        compiler_params=_params(("arbitrary",)),
        name="in_proj",
    )(x2d, nw, wz, wxbc, why, wdth, wdtl)


def _shifted(xg, o):
    if o == 0:
        return xg
    r = pltpu.roll(xg, (-o) % SUBLANE, axis=1)
    sub = lax.broadcasted_iota(jnp.int32, (1, SUBLANE, 1), 1)
    if o < 0:
        return jnp.where(sub < -o, jnp.concatenate([r[:1], r[:-1]], axis=0), r)
    return jnp.where(sub >= SUBLANE - o, jnp.concatenate([r[1:], r[-1:]], axis=0), r)


def _dwconv(xm_ref, xp_ref, xn_ref, cw_ref, cb_ref, cols, width, first, last):
    TL = xm_ref.shape[0]
    W = cols.stop - cols.start
    xp = jnp.where(first, 0.0, xp_ref[:, cols])
    xn = jnp.where(last, 0.0, xn_ref[:, cols])
    xg = jnp.concatenate([xp, xm_ref[:, cols], xn], axis=0).reshape(TL // SUBLANE + 2, SUBLANE, W)
    acc = cb_ref[:, cols].reshape(1, 1, W)
    for j in range(width):
        acc = acc + cw_ref[j:j + 1, cols].reshape(1, 1, W) * _shifted(xg, j - width // 2)
    return acc[1:-1].reshape(TL, W)


def _conv_kernel(xm_ref, xp_ref, xn_ref, hm_ref, hp_ref, hn_ref, scw_ref, scb_ref, hcw_ref, hcb_ref,
                 xs_ref, bc_ref, g_ref, x0_ref, *, ntiles, ssd_w, hy_w, cc):
    i = pl.program_id(1)
    first, last = i == 0, i == ntiles - 1
    W = xs_ref.shape[1]
    C = g_ref.shape[1]
    for c0 in range(0, xm_ref.shape[1], cc):
        act = _silu(_dwconv(xm_ref, xp_ref, xn_ref, scw_ref, scb_ref, slice(c0, c0 + cc), ssd_w, first, last))
        if c0 < W:
            xs_ref[:, c0:c0 + cc] = act
        else:
            bc_ref[:, c0 - W:c0 - W + cc] = act.astype(BF16)
    for c0 in range(0, C, cc):
        conv = lambda off: _dwconv(hm_ref, hp_ref, hn_ref, hcw_ref, hcb_ref,
                                   slice(off + c0, off + c0 + cc), hy_w, first, last)
        x0_ref[:, c0:c0 + cc] = conv(0)
        g_ref[:, c0:c0 + cc] = conv(2 * C) * conv(C)


def _conv(xbc, hy, ssd_cw, ssd_cb, hy_cw, hy_cb, batch, seqlen, tl=256, cc=512):
    R, XW = xbc.shape
    HW = hy.shape[1]
    C = HW // 3
    W = SSD_HEADS * SSD_HEAD_DIM
    pad_w = lambda w: jnp.zeros((SUBLANE, w.shape[1]), F32).at[:w.shape[0]].set(w)
    nt = seqlen // tl
    tb = tl // SUBLANE
    rows8 = R // SUBLANE
    blk = lambda b, i: b * nt + i
    main = lambda w: pl.BlockSpec((tl, w), lambda b, i: (blk(b, i), 0))
    prev = lambda w: pl.BlockSpec((SUBLANE, w), lambda b, i: (jnp.maximum(blk(b, i) * tb - 1, 0), 0))
    nxt = lambda w: pl.BlockSpec((SUBLANE, w), lambda b, i: (jnp.minimum((blk(b, i) + 1) * tb, rows8 - 1), 0))
    kern = functools.partial(_conv_kernel, ntiles=nt, ssd_w=ssd_cw.shape[0], hy_w=hy_cw.shape[0], cc=cc)
    return pl.pallas_call(
        kern,
        grid=(batch, nt),
        in_specs=[main(XW), prev(XW), nxt(XW), main(HW), prev(HW), nxt(HW),
                  _const_spec((SUBLANE, XW)), _const_spec((1, XW)),
                  _const_spec((SUBLANE, HW)), _const_spec((1, HW))],
        out_specs=[main(W), main(XW - W), main(C), main(C)],
        out_shape=[jax.ShapeDtypeStruct((R, W), F32), jax.ShapeDtypeStruct((R, XW - W), BF16),
                   jax.ShapeDtypeStruct((R, C), F32), jax.ShapeDtypeStruct((R, C), F32)],
        compiler_params=_params(("arbitrary", "arbitrary")),
        name="conv",
    )(xbc, xbc, xbc, hy, hy, hy, pad_w(ssd_cw), ssd_cb.reshape(1, XW), pad_w(hy_cw), hy_cb.reshape(1, HW))


def _ssd_kernel(xs_ref, bc_ref, dt_ref, bias_ref, alog_ref, dskip_ref, tri_ref, y_ref, state_ref):
    d = pl.program_id(1)
    i = pl.program_id(2)
    T = xs_ref.shape[0]
    W = SSD_HEADS * SSD_HEAD_DIM
    GN = SSD_GROUPS * SSD_STATE
    GW = W // SSD_GROUPS

    @pl.when(i == 0)
    def _():
        state_ref[...] = jnp.zeros_like(state_ref)

    dtv = _softplus(dt_ref[...] + bias_ref[...])
    a = dtv * (-jnp.exp(alog_ref[...]))
    tri = tri_ref[...]
    mask = tri > 0.5
    cs = jnp.dot(tri, a, precision=HIGHEST, preferred_element_type=F32)
    tot = jnp.sum(a, axis=0, keepdims=True)
    ecs = jnp.exp(cs)
    wend = dtv * jnp.exp(tot - cs)
    cdec = jnp.exp(tot)
    csT = cs.T
    dtT = dtv.T
    dskip = jnp.where(d == 0, dskip_ref[...], 0.0)
    lo_half = lax.broadcasted_iota(jnp.int32, (T, LANE), 1) < SSD_HEAD_DIM
    col = lambda v, h: jnp.broadcast_to(v[:, h:h + 1], (T, LANE))
    pair = lambda v, h: jnp.where(lo_half, col(v, h), col(v, h + 1))

    heads_per_group = SSD_HEADS // SSD_GROUPS
    for g in range(SSD_GROUPS):
        cg = bc_ref[:, GN + g * SSD_STATE:GN + (g + 1) * SSD_STATE]
        bg = bc_ref[:, g * SSD_STATE:(g + 1) * SSD_STATE]
        gl = slice(g * GW, (g + 1) * GW)
        scores = lax.dot_general(cg, bg, (((1,), (1,)), ((), ())), preferred_element_type=F32)
        st = state_ref[:, gl]
        yoff = jnp.dot(cg, st.astype(BF16), preferred_element_type=F32)
        xdec = []
        cdec_g = []
        for pr in range(heads_per_group // 2):
            h0 = g * heads_per_group + 2 * pr
            ls = slice(h0 * SSD_HEAD_DIM, h0 * SSD_HEAD_DIM + LANE)
            xpair = xs_ref[:, ls]
            yp = yoff[:, pr * LANE:(pr + 1) * LANE] * pair(ecs, h0) + xpair * dskip[:, ls]
            for k in range(2):
                h = h0 + k
                seg = col(cs, h) - csT[h:h + 1, :]
                m = (scores * jnp.exp(jnp.where(mask, seg, -1e30)) * dtT[h:h + 1, :]).astype(BF16)
                xk = jnp.where(lo_half if k == 0 else ~lo_half, xpair, 0.0).astype(BF16)
                yp = yp + jnp.dot(m, xk, preferred_element_type=F32)
            y_ref[:, ls] = yp
            xdec.append((xpair * pair(wend, h0)).astype(BF16))
            cdec_g.append(jnp.where(lo_half[0:1], cdec[:, h0:h0 + 1], cdec[:, h0 + 1:h0 + 2]))
        upd = lax.dot_general(bg, jnp.concatenate(xdec, axis=1), (((0,), (0,)), ((), ())),
                              preferred_element_type=F32)
        state_ref[:, gl] = st * jnp.concatenate(cdec_g, axis=1) + upd


def _ssd(xs, bc, dt, dt_bias, a_log, ssd_d, batch, seqlen):
    R, W = xs.shape
    T = SSD_CHUNK
    c = seqlen // T
    pad = lambda v: jnp.zeros((2, 1, LANE), F32).at[:, 0, :SSD_HEADS].set(v)
    dskip = jnp.repeat(ssd_d, SSD_HEAD_DIM).reshape(1, W)
    lower = np.tril(np.ones((T, T), np.float32))
    tri = jnp.asarray(np.stack([lower, lower.T]))

    def blk(b, d, i):
        return b * c + i + d * (c - 1 - 2 * i)

    return pl.pallas_call(
        _ssd_kernel,
        grid=(batch, 2, c),
        in_specs=[
            pl.BlockSpec((T, W), lambda b, d, i: (blk(b, d, i), 0)),
            pl.BlockSpec((T, bc.shape[1]), lambda b, d, i: (blk(b, d, i), 0)),
            pl.BlockSpec((T, LANE), lambda b, d, i: (blk(b, d, i), d)),
            pl.BlockSpec((None, 1, LANE), lambda b, d, i: (d, 0, 0)),
            pl.BlockSpec((None, 1, LANE), lambda b, d, i: (d, 0, 0)),
            _const_spec(dskip.shape),
            pl.BlockSpec((None, T, T), lambda b, d, i: (d, 0, 0)),
        ],
        out_specs=pl.BlockSpec((None, T, W), lambda b, d, i: (d, blk(b, d, i), 0)),
        out_shape=jax.ShapeDtypeStruct((2, R, W), F32),
        scratch_shapes=[pltpu.VMEM((SSD_STATE, W), F32)],
        compiler_params=_params(("arbitrary", "arbitrary", "arbitrary")),
        name="ssd",
    )(xs, bc, dt, pad(dt_bias), pad(a_log), dskip, tri)


def _filt_gen_kernel(fb_ref, w1h_ref, w1l_ref, b1_ref, w2h_ref, w2l_ref, b2_ref, w3h_ref, w3l_ref, b3_ref,
                     w4h_ref, w4l_ref, fr_ref, dl_ref, hs_ref, hd_ref, *, seqlen, cc):
    TF, C = hs_ref.shape
    L = seqlen
    n = pl.program_id(0) * TF + lax.broadcasted_iota(jnp.int32, (TF, 1), 0)
    pos = n.astype(F32)
    t = pos / (L - 1)
    lane = lax.broadcasted_iota(jnp.int32, (1, LANE), 1)
    bands = (HY_EMB - 1) // 2
    ang = fb_ref[...] * (2.0 * math.pi * pos / L)
    z = jnp.where(lane == 0, t,
                  jnp.where(lane <= bands, jnp.cos(ang),
                            jnp.where(lane <= 2 * bands, -jnp.sin(ang), 0.0)))
    fr = fr_ref[...]
    h = jnp.sin(fr * (_dot3(z, w1h_ref[...], w1l_ref[...]) + b1_ref[...]))
    h = jnp.sin(fr * (_dot3(h, w2h_ref[...], w2l_ref[...]) + b2_ref[...]))
    h = jnp.sin(fr * (_dot3(h, w3h_ref[...], w3l_ref[...]) + b3_ref[...]))
    for c0 in range(0, C, cc):
        win = jnp.exp(-t * dl_ref[:, c0:c0 + cc])
        hf = _dot3(h, w4h_ref[:, c0:c0 + cc], w4l_ref[:, c0:c0 + cc]) * win
        hb = _dot3(h, w4h_ref[:, C + c0:C + c0 + cc], w4l_ref[:, C + c0:C + c0 + cc]) * win
        hb = jnp.where(n == 0, 0.0, hb)
        hs_ref[:, c0:c0 + cc] = hf + hb
        hd_ref[:, c0:c0 + cc] = hf - hb


def _filt_gen(seqlen, w1, b1, w2, b2, w3, b3, w4, freq, tf=512, cc=512):
    L = seqlen
    hidden = w1.shape[1]
    C = w4.shape[1] // 2
    bands = (HY_EMB - 1) // 2
    fb = np.zeros((1, LANE), np.float64)
    fbv = np.linspace(1e-4, bands - 1, bands)
    fb[0, 1:1 + bands] = fbv
    fb[0, 1 + bands:1 + 2 * bands] = fbv
    max_decay = math.log(HY_DECAY_TARGET) / HY_FAST_DECAY
    min_decay = math.log(HY_DECAY_TARGET) / HY_SLOW_DECAY
    dl = np.abs(np.linspace(min_decay, max_decay, C)).reshape(1, C)
    w1p = jnp.zeros((LANE, hidden), F32).at[:HY_EMB].set(w1)
    row = lambda v: v.reshape(1, hidden)
    args = (jnp.asarray(fb, F32), *_split_bf16(w1p), row(b1), *_split_bf16(w2), row(b2),
            *_split_bf16(w3), row(b3), *_split_bf16(w4), row(freq), jnp.asarray(dl, F32))
    return pl.pallas_call(
        functools.partial(_filt_gen_kernel, seqlen=L, cc=cc),
        grid=(L // tf,),
        in_specs=[_const_spec(a.shape) for a in args],
        out_specs=[pl.BlockSpec((tf, C), lambda i: (i, 0))] * 2,
        out_shape=[jax.ShapeDtypeStruct((L, C), F32)] * 2,
        compiler_params=_params(("arbitrary",)),
        name="filt_gen",
    )(*args)


@functools.lru_cache(maxsize=None)
def _dft_tables(seqlen):
    N = 2 * seqlen
    N1 = DFT_N1
    N2 = N // N1
    K = N1 // 2
    k1 = np.arange(N1).reshape(1, N1, 1)
    n1 = np.arange(K).reshape(1, 1, K)
    n2 = np.arange(N2).reshape(N2, 1, 1)
    ang = -2.0 * np.pi * ((n2 * k1 + N2 * n1 * k1) % N) / N
    gr, gi = np.cos(ang), np.sin(ang)
    gk = np.concatenate([gr, gi], axis=1)
    ga = np.concatenate([np.concatenate([gr, -gi], axis=2),
                         np.concatenate([gi, gr], axis=2)], axis=1)
    grt, git = np.swapaxes(gr, 1, 2), np.swapaxes(gi, 1, 2)
    gc = np.concatenate([np.concatenate([grt, git], axis=2),
                         np.concatenate([-git, grt], axis=2)], axis=1)
    k2 = np.arange(N2).reshape(N2, 1)
    m2 = np.arange(N2).reshape(1, N2)
    ang2 = -2.0 * np.pi * ((k2 * m2) % N2) / N2
    fr, fi = np.cos(ang2), np.sin(ang2)
    fb = np.block([[fr, -fi], [fi, fr]])
    fbc = np.block([[fr, fi], [-fi, fr]])
    to = lambda x: x.astype(np.float32)
    return to(gk), to(ga), to(gc), to(fb), to(fbc)


def _mxu_tables(seqlen):
    return tuple(jnp.asarray(t).astype(BF16) for t in _dft_tables(seqlen))


def _stack_bf16(re, im):
    return jnp.concatenate([re, im], axis=0).astype(BF16)


def _filt_a_kernel(hs_ref, hd_ref, g_ref, o_ref):
    N1 = o_ref.shape[2]
    for r in range(SUBLANE):
        for q, h_ref in enumerate((hs_ref, hd_ref)):
            y = jnp.dot(g_ref[r], h_ref[:, r, :].astype(BF16), preferred_element_type=F32)
            o_ref[q, 0, :, r, :] = y[:N1]
            o_ref[q, 1, :, r, :] = y[N1:]


def _filt_b_kernel(a_ref, f_ref, o_ref, *, kb, C, cc, scale):
    N2 = a_ref.shape[3]
    for q in range(kb):
        for c0 in range(0, C, cc):
            cols = slice(c0, c0 + cc)
            for part in range(2):
                x = _stack_bf16(a_ref[part, 0, q, :, cols], a_ref[part, 1, q, :, cols])
                z = jnp.dot(f_ref[part * N2:(part + 1) * N2, :], x, preferred_element_type=F32)
                o_ref[part, q, :, cols] = z * scale


def _filter_spectrum(hs, hd, seqlen, kb, cb=512, cc=512):
    L, C = hs.shape
    N = 2 * L
    N1 = DFT_N1
    N2 = N // N1
    K = N1 // 2
    gk, _, _, fb, _ = _mxu_tables(seqlen)
    hspec = pl.BlockSpec((K, SUBLANE, cb), lambda j, c: (0, j, c))
    ka = pl.pallas_call(
        _filt_a_kernel,
        grid=(N2 // SUBLANE, C // cb),
        in_specs=[hspec, hspec, pl.BlockSpec((SUBLANE, 2 * N1, K), lambda j, c: (j, 0, 0))],
        out_specs=pl.BlockSpec((2, 2, N1, SUBLANE, cb), lambda j, c: (0, 0, 0, j, c)),
        out_shape=jax.ShapeDtypeStruct((2, 2, N1, N2, C), F32),
        compiler_params=_params(("arbitrary", "arbitrary")),
        name="filt_a",
    )(hs.reshape(K, N2, C), hd.reshape(K, N2, C), gk)
    return pl.pallas_call(
        functools.partial(_filt_b_kernel, kb=kb, C=C, cc=cc, scale=1.0 / N),
        grid=(N1 // kb,),
        in_specs=[pl.BlockSpec((2, 2, kb, N2, C), lambda j: (0, 0, j, 0, 0)), _const_spec(fb.shape)],
        out_specs=pl.BlockSpec((2, kb, N2, C), lambda j: (0, j, 0, 0)),
        out_shape=jax.ShapeDtypeStruct((2, N1, N2, C), F32),
        compiler_params=_params(("arbitrary",)),
        name="filt_b",
    )(ka, fb)


def _fft_a_kernel(x_ref, g_ref, o_ref):
    N1 = o_ref.shape[1]
    for r in range(SUBLANE):
        y = jnp.dot(g_ref[r], _stack_bf16(x_ref[0, :, r, :], x_ref[1, :, r, :]),
                    preferred_element_type=F32)
        o_ref[0, :, r, :] = y[:N1]
        o_ref[1, :, r, :] = y[N1:]


def _fft_b_kernel(a_ref, kh_ref, f_ref, fc_ref, o_ref, *, kb, C, cc):
    N2 = a_ref.shape[2]
    for q in range(kb):
        for c0 in range(0, C, cc):
            cols = slice(c0, c0 + cc)
            z = jnp.dot(f_ref[...], _stack_bf16(a_ref[0, q, :, cols], a_ref[1, q, :, cols]),
                        preferred_element_type=F32)
            zr, zi = z[:N2], z[N2:]
            kr, ki = kh_ref[0, q, :, cols], kh_ref[1, q, :, cols]
            v = jnp.dot(fc_ref[...], _stack_bf16(zr * kr - zi * ki, zr * ki + zi * kr),
                        preferred_element_type=F32)
            o_ref[0, q, :, cols] = v[:N2]
            o_ref[1, q, :, cols] = v[N2:]


def _fft_c_kernel(bt_ref, gc_ref, g_ref, x0_ref, bias_ref, nw_ref, avg_ref, o_ref, conv_ref):
    _, K, _, cb = o_ref.shape
    for r in range(SUBLANE):
        y = jnp.dot(gc_ref[r], _stack_bf16(bt_ref[0, :, r, :], bt_ref[1, :, r, :]),
                    preferred_element_type=F32)
        conv_ref[0, :, r, :] = y[:K]
        conv_ref[1, :, r, :] = y[K:]
    rows = 2 * K * SUBLANE
    flat = lambda ref: ref[...].reshape(rows, cb)
    gg = flat(g_ref)
    v = flat(x0_ref) * (flat(conv_ref) + gg * bias_ref[...])
    hi, lo = _split_bf16(v * v)
    ms = jnp.concatenate(
        [jnp.dot(hi[:, m:m + MXU_DIM], avg_ref[...], preferred_element_type=F32)
         + jnp.dot(lo[:, m:m + MXU_DIM], avg_ref[...], preferred_element_type=F32)
         for m in range(0, cb, MXU_DIM)], axis=1)
    o_ref[...] = (v * lax.rsqrt(ms + EPS) * nw_ref[...]).reshape(o_ref.shape)


def _long_conv(g, x0, kh, hy_bias, hy_norm_w, batch, seqlen, kb, cb=512, cc=512):
    R, C = g.shape
    L = seqlen
    N1 = DFT_N1
    N2 = 2 * L // N1
    K = N1 // 2
    bp = batch // 2
    _, ga, gc, fb, fbc = _mxu_tables(L)
    gv = g.reshape(bp, 2, K, N2, C)
    x0v = x0.reshape(bp, 2, K, N2, C)
    half = pl.BlockSpec((None, 2, K, SUBLANE, cb), lambda p, j, c: (p, 0, 0, j, c))
    full = pl.BlockSpec((None, 2, N1, SUBLANE, cb), lambda p, j, c: (p, 0, 0, j, c))
    grid_ac = (bp, N2 // SUBLANE, C // cb)
    a = pl.pallas_call(
        _fft_a_kernel,
        grid=grid_ac,
        in_specs=[half, pl.BlockSpec((SUBLANE, 2 * N1, N1), lambda p, j, c: (j, 0, 0))],
        out_specs=full,
        out_shape=jax.ShapeDtypeStruct((bp, 2, N1, N2, C), F32),
        compiler_params=_params(("arbitrary",) * 3),
        name="fft_a",
    )(gv, ga)
    bt = pl.pallas_call(
        functools.partial(_fft_b_kernel, kb=kb, C=C, cc=cc),
        grid=(N1 // kb, bp),
        in_specs=[pl.BlockSpec((None, 2, kb, N2, C), lambda j, p: (p, 0, j, 0, 0)),
                  pl.BlockSpec((2, kb, N2, C), lambda j, p: (0, j, 0, 0)),
                  _const_spec(fb.shape), _const_spec(fbc.shape)],
        out_specs=pl.BlockSpec((None, 2, kb, N2, C), lambda j, p: (p, 0, j, 0, 0)),
        out_shape=jax.ShapeDtypeStruct((bp, 2, N1, N2, C), F32),
        compiler_params=_params(("arbitrary", "arbitrary")),
        name="fft_b",
    )(a, kh, fb, fbc)
    avg = np.kron(np.eye(MXU_DIM // HY_GROUP), np.full((HY_GROUP, HY_GROUP), 1.0 / HY_GROUP))
    vec = pl.BlockSpec((1, cb), lambda p, j, c: (0, c))
    y = pl.pallas_call(
        _fft_c_kernel,
        grid=grid_ac,
        in_specs=[full, pl.BlockSpec((SUBLANE, N1, 2 * N1), lambda p, j, c: (j, 0, 0)),
                  half, half, vec, vec, _const_spec((MXU_DIM, MXU_DIM))],
        out_specs=half,
        out_shape=jax.ShapeDtypeStruct((bp, 2, K, N2, C), F32),
        scratch_shapes=[pltpu.VMEM((2, K, SUBLANE, cb), F32)],
        compiler_params=_params(("arbitrary",) * 3),
        name="fft_c",
    )(bt, gc, gv, x0v, hy_bias.reshape(1, C), hy_norm_w.reshape(1, C),
      jnp.asarray(avg.astype(np.float32)).astype(BF16))
    return y.reshape(R, C)


def _mix_kernel(yf_ref, yb_ref, z_ref, yhy_ref, h_ref, p_ref, snw_ref, wos_ref, woh_ref, nmp_ref,
                nfp_ref, wg_ref, wu_ref, wd_ref, nfo_ref, npp_ref, wpg_ref, wpp_ref, npo_ref, o_ref):
    dot = lambda x, w_ref: jnp.dot(x.astype(BF16), w_ref[...], preferred_element_type=F32)
    ys = (yf_ref[...] + yb_ref[...]) * _silu(z_ref[...])
    gw = ys.shape[1] // SSD_GROUPS
    ysn = jnp.concatenate(
        [_rms(ys[:, k * gw:(k + 1) * gw], snw_ref[:, k * gw:(k + 1) * gw]) for k in range(SSD_GROUPS)],
        axis=1)
    mix = dot(ysn, wos_ref) + dot(yhy_ref[...], woh_ref)
    h1 = h_ref[...] + _rms(mix, nmp_ref[...])
    u = _rms(h1, nfp_ref[...])
    ff = dot(_silu(dot(u, wg_ref)) * dot(u, wu_ref), wd_ref)
    h2 = h1 + _rms(ff, nfo_ref[...])
    gate = jax.nn.sigmoid(dot(_rms(h2, npp_ref[...]), wpg_ref))
    emb = dot(p_ref[...], wpp_ref)
    o_ref[...] = h2 + _rms(gate * emb, npo_ref[...])


def _mix(y2, z, yhy, h, p, consts, tm=256):
    R, D = h.shape
    row = lambda w: pl.BlockSpec((tm, w), lambda i: (i, 0))
    return pl.pallas_call(
        _mix_kernel,
        grid=(R // tm,),
        in_specs=[
            pl.BlockSpec((None, tm, D), lambda i: (0, i, 0)),
            pl.BlockSpec((None, tm, D), lambda i: (1, i, 0)),
            row(D), row(D), row(D), row(p.shape[1]),
        ] + [_const_spec(c.shape) for c in consts],
        out_specs=row(D),
        out_shape=jax.ShapeDtypeStruct((R, D), F32),
        compiler_params=_params(("arbitrary",)),
        name="mix",
    )(y2, y2, z, yhy, h, p, *consts)


def _layer(x, p, w):
    batch, seqlen, D = x.shape
    R = batch * seqlen
    x2d = x.reshape(R, D)
    row = lambda v: v.reshape(1, -1)
    o1 = SSD_HEADS * SSD_HEAD_DIM
    o2 = o1 + w["ssd_conv_w"].shape[1]
    o3 = o2 + 2 * SSD_HEADS
    w_in = w["w_in"]
    wdt = jnp.zeros((D, 2 * LANE), F32)
    wdt = wdt.at[:, :SSD_HEADS].set(w_in[:, o2:o2 + SSD_HEADS])
    wdt = wdt.at[:, LANE:LANE + SSD_HEADS].set(w_in[:, o2 + SSD_HEADS:o3])
    z, xbc, hy, dt = _in_proj(
        x2d, row(w["norm_mix_pre"]), w_in[:, :o1].astype(BF16), w_in[:, o1:o2].astype(BF16),
        w_in[:, o3:].astype(BF16), wdt)

    xs, bc, g, x0 = _conv(xbc, hy, w["ssd_conv_w"], w["ssd_conv_b"], w["hy_conv_w"], w["hy_conv_b"],
                          batch, seqlen)
    y2 = _ssd(xs, bc, dt, w["ssd_dt_bias"], w["ssd_a_log"], w["ssd_d"], batch, seqlen)

    n2 = 2 * seqlen // DFT_N1
    kb = max(1, min(DFT_N1, 256 // n2))
    hs, hd = _filt_gen(seqlen, w["hy_f_w1"], w["hy_f_b1"], w["hy_f_w2"], w["hy_f_b2"], w["hy_f_w3"],
                       w["hy_f_b3"], w["hy_f_w4"], w["hy_f_freq"], tf=min(512, seqlen))
    kh = _filter_spectrum(hs, hd, seqlen, kb)
    yhy = _long_conv(g, x0, kh, w["hy_bias"], w["hy_norm_w"], batch, seqlen, kb)

    w_out = w["w_out"].astype(BF16)
    consts = [row(w["ssd_norm_w"]), w_out[:o1], w_out[o1:], row(w["norm_mix_post"]),
              row(w["norm_ffn_pre"]), w["w_gate"].astype(BF16), w["w_up"].astype(BF16),
              w["w_down"].astype(BF16), row(w["norm_ffn_post"]), row(w["ple_norm_pre"]),
              w["w_ple_gate"].astype(BF16), w["w_ple_proj"].astype(BF16), row(w["ple_norm_post"])]
    out = _mix(y2, z, yhy, x2d, p.reshape(R, -1), consts)
    return out.reshape(batch, seqlen, D)


_WEIGHT_NAMES = (
    "norm_mix_pre", "w_in", "ssd_conv_w", "ssd_conv_b", "ssd_dt_bias", "ssd_a_log", "ssd_d",
    "ssd_norm_w", "hy_conv_w", "hy_conv_b", "hy_f_w1", "hy_f_b1", "hy_f_w2", "hy_f_b2", "hy_f_w3",
    "hy_f_b3", "hy_f_w4", "hy_f_freq", "hy_bias", "hy_norm_w", "w_out", "norm_mix_post",
    "norm_ffn_pre", "w_gate", "w_up", "w_down", "norm_ffn_post", "ple_norm_pre", "w_ple_gate",
    "w_ple_proj", "ple_norm_post")


def _trunk(x, p, weights):
    h = x
    for i in range(p.shape[0]):
        h = _layer(h, p[i], {n: v[i] for n, v in zip(_WEIGHT_NAMES, weights)})
    return h


def kernel(x_prompt, x_sample, p_prompt, p_sample, norm_mix_pre, w_in, ssd_conv_w, ssd_conv_b, ssd_dt_bias, ssd_a_log, ssd_d, ssd_norm_w, hy_conv_w, hy_conv_b, hy_f_w1, hy_f_b1, hy_f_w2, hy_f_b2, hy_f_w3, hy_f_b3, hy_f_w4, hy_f_freq, hy_bias, hy_norm_w, w_out, norm_mix_post, norm_ffn_pre, w_gate, w_up, w_down, norm_ffn_post, ple_norm_pre, w_ple_gate, w_ple_proj, ple_norm_post):
    weights = (norm_mix_pre, w_in, ssd_conv_w, ssd_conv_b, ssd_dt_bias, ssd_a_log, ssd_d, ssd_norm_w,
               hy_conv_w, hy_conv_b, hy_f_w1, hy_f_b1, hy_f_w2, hy_f_b2, hy_f_w3, hy_f_b3, hy_f_w4,
               hy_f_freq, hy_bias, hy_norm_w, w_out, norm_mix_post, norm_ffn_pre, w_gate, w_up,
               w_down, norm_ffn_post, ple_norm_pre, w_ple_gate, w_ple_proj, ple_norm_post)
    return (_trunk(x_prompt, p_prompt, weights), _trunk(x_sample, p_sample, weights))
```

```python
import functools
import math

import numpy as np
import jax
import jax.numpy as jnp
from jax import lax
from jax.experimental import pallas as pl
from jax.experimental.pallas import tpu as pltpu

F32 = jnp.float32
BF16 = jnp.bfloat16
HIGHEST = lax.Precision.HIGHEST

EPS = 1e-6
LANE = 128
SUBLANE = 8
MXU_DIM = 256
VMEM_LIMIT = 56 * 1024 * 1024

SSD_HEAD_DIM = 64
SSD_HEADS = 16
SSD_GROUPS = 2
SSD_STATE = 128
SSD_CHUNK = 128
HY_GROUP = 64
HY_EMB = 33
HY_FAST_DECAY = 0.3
HY_SLOW_DECAY = 1.5
HY_DECAY_TARGET = 1e-2
DFT_N1 = 128
HALO = 16


def _params(sem):
    return pltpu.CompilerParams(dimension_semantics=sem, vmem_limit_bytes=VMEM_LIMIT)


def _const_spec(shape):
    nd = len(shape)
    return pl.BlockSpec(shape, lambda *_: (0,) * nd, pipeline_mode=pl.Buffered(1))


def _rms(x, w):
    return x * lax.rsqrt(jnp.mean(x * x, axis=-1, keepdims=True) + EPS) * w


def _silu(x):
    return x * jax.nn.sigmoid(x)


def _softplus(x):
    return jnp.maximum(x, 0.0) + jnp.log1p(jnp.exp(-jnp.abs(x)))


def _split_bf16(x):
    hi = x.astype(BF16)
    return hi, (x - hi.astype(F32)).astype(BF16)


def _dot3(x, w_hi, w_lo):
    x_hi, x_lo = _split_bf16(x)
    dot = lambda a, b: jnp.dot(a, b, preferred_element_type=F32)
    return dot(x_hi, w_hi) + (dot(x_hi, w_lo) + dot(x_lo, w_hi))


def _shifted(xg, o):
    if o == 0:
        return xg
    r = pltpu.roll(xg, (-o) % SUBLANE, axis=1)
    sub = lax.broadcasted_iota(jnp.int32, (1, SUBLANE, 1), 1)
    if o < 0:
        return jnp.where(sub < -o, jnp.concatenate([r[:1], r[:-1]], axis=0), r)
    return jnp.where(sub >= SUBLANE - o, jnp.concatenate([r[1:], r[-1:]], axis=0), r)


def _dwconv(p_ext, cw_ref, cb_ref, cols, width):
    rows, W = p_ext.shape
    xg = p_ext.reshape(rows // SUBLANE, SUBLANE, W)
    acc = cb_ref[:, cols].reshape(1, 1, W)
    for j in range(width):
        acc = acc + cw_ref[j:j + 1, cols].reshape(1, 1, W) * _shifted(xg, j - width // 2)
    hg = HALO // SUBLANE
    return acc[hg:-hg].reshape(rows - 2 * HALO, W)


def _in_proj_kernel(xm_ref, xp_ref, xn_ref, nw_ref, wz_ref, wxbc_ref, why_ref, wdth_ref, wdtl_ref, dtb_ref,
                    scw_ref, scb_ref, hcw_ref, hcb_ref, z_ref, dt_ref, xs_ref, bc_ref, g_ref, x0_ref,
                    *, ntiles, ssd_w, hy_w, cc):
    i = pl.program_id(1)
    tm = xm_ref.shape[0]
    W = xs_ref.shape[1]
    C = g_ref.shape[1]
    xp = jnp.where(i == 0, 0.0, xp_ref[...])
    xn = jnp.where(i == ntiles - 1, 0.0, xn_ref[...])
    u = _rms(jnp.concatenate([xp, xm_ref[...], xn], axis=0), nw_ref[...])
    ub, ul = _split_bf16(u)
    ubm, ulm = ub[HALO:HALO + tm], ul[HALO:HALO + tm]
    dot = lambda a, b: jnp.dot(a, b, preferred_element_type=F32)
    for j in range(0, W, cc):
        z_ref[:, j:j + cc] = dot(ubm, wz_ref[:, j:j + cc]).astype(z_ref.dtype)
    dt_raw = dot(ubm, wdth_ref[...]) + (dot(ubm, wdtl_ref[...]) + dot(ulm, wdth_ref[...]))
    dt_ref[...] = _softplus(dt_raw + dtb_ref[...])
    for c0 in range(0, wxbc_ref.shape[1], cc):
        cols = slice(c0, c0 + cc)
        act = _silu(_dwconv(dot(ub, wxbc_ref[:, cols]), scw_ref, scb_ref, cols, ssd_w))
        if c0 < W:
            xs_ref[:, cols] = act
        else:
            bc_ref[:, c0 - W:c0 - W + cc] = act.astype(BF16)
    for c0 in range(0, C, cc):
        conv = lambda off: _dwconv(dot(ub, why_ref[:, off + c0:off + c0 + cc]), hcw_ref, hcb_ref,
                                   slice(off + c0, off + c0 + cc), hy_w)
        x0_ref[:, c0:c0 + cc] = conv(0)
        g_ref[:, c0:c0 + cc] = conv(2 * C) * conv(C)


def _in_proj(x2d, nw, wz, wxbc, why, wdt, dt_bias, ssd_cw, ssd_cb, hy_cw, hy_cb, batch, seqlen, tm=512, cc=512):
    R, D = x2d.shape
    W = wz.shape[1]
    XW = wxbc.shape[1]
    HW = why.shape[1]
    C = HW // 3
    wdth, wdtl = _split_bf16(wdt)
    pad_w = lambda w: jnp.zeros((SUBLANE, w.shape[1]), F32).at[:w.shape[0]].set(w)
    nt = seqlen // tm
    hb = tm // HALO
    nhalo = R // HALO
    blk = lambda b, i: b * nt + i
    main = lambda w: pl.BlockSpec((tm, w), lambda b, i: (blk(b, i), 0))
    consts = (nw, wz, wxbc, why, wdth, wdtl, dt_bias, pad_w(ssd_cw), ssd_cb.reshape(1, XW),
              pad_w(hy_cw), hy_cb.reshape(1, HW))
    kern = functools.partial(_in_proj_kernel, ntiles=nt, ssd_w=ssd_cw.shape[0], hy_w=hy_cw.shape[0], cc=cc)
    return pl.pallas_call(
        kern,
        grid=(batch, nt),
        in_specs=[main(D),
                  pl.BlockSpec((HALO, D), lambda b, i: (jnp.maximum(blk(b, i) * hb - 1, 0), 0)),
                  pl.BlockSpec((HALO, D), lambda b, i: (jnp.minimum((blk(b, i) + 1) * hb, nhalo - 1), 0))]
        + [_const_spec(a.shape) for a in consts],
        out_specs=[main(W), main(wdt.shape[1]), main(W), main(XW - W), main(C), main(C)],
        out_shape=[jax.ShapeDtypeStruct((R, W), BF16), jax.ShapeDtypeStruct((R, wdt.shape[1]), F32),
                   jax.ShapeDtypeStruct((R, W), F32), jax.ShapeDtypeStruct((R, XW - W), BF16),
                   jax.ShapeDtypeStruct((R, C), F32), jax.ShapeDtypeStruct((R, C), F32)],
        compiler_params=_params(("arbitrary", "arbitrary")),
        name="in_proj",
    )(x2d, x2d, x2d, *consts)


def _ssd_chunk(xs_ref, bc_ref, dt_ref, alog_ref, dskip_ref, tri, y_ref, state_ref):
    T = xs_ref.shape[0]
    W = SSD_HEADS * SSD_HEAD_DIM
    GN = SSD_GROUPS * SSD_STATE
    GW = W // SSD_GROUPS
    dtv = dt_ref[...]
    a = dtv * (-jnp.exp(alog_ref[...]))
    mask = tri > 0.5
    cs = jnp.dot(tri, a, precision=HIGHEST, preferred_element_type=F32)
    tot = jnp.sum(a, axis=0, keepdims=True)
    cdec = jnp.exp(tot)
    csT = cs.T
    lo_half = lax.broadcasted_iota(jnp.int32, (T, LANE), 1) < SSD_HEAD_DIM
    lo_row = lo_half[0:1]
    col = lambda v, h: jnp.broadcast_to(v[:, h:h + 1], (v.shape[0], LANE))

    heads_per_group = SSD_HEADS // SSD_GROUPS
    for g in range(SSD_GROUPS):
        cg = bc_ref[:, GN + g * SSD_STATE:GN + (g + 1) * SSD_STATE]
        bg = bc_ref[:, g * SSD_STATE:(g + 1) * SSD_STATE]
        gl = slice(g * GW, (g + 1) * GW)
        scores = lax.dot_general(cg, bg, (((1,), (1,)), ((), ())), preferred_element_type=F32)
        st = state_ref[:, gl]
        yoff = jnp.dot(cg, st.astype(BF16), preferred_element_type=F32)
        xdec = []
        cdec_g = []
        for pr in range(heads_per_group // 2):
            h0 = g * heads_per_group + 2 * pr
            ls = slice(h0 * SSD_HEAD_DIM, h0 * SSD_HEAD_DIM + LANE)
            xpair = xs_ref[:, ls]
            bcs = [col(cs, h0), col(cs, h0 + 1)]
            bcs_p = jnp.where(lo_half, bcs[0], bcs[1])
            pidx = jnp.where(lo_half, h0, h0 + 1)
            xdt = xpair * jnp.take_along_axis(dtv, pidx, axis=1)
            tot_p = jnp.where(lo_row, col(tot, h0), col(tot, h0 + 1))
            yp = yoff[:, pr * LANE:(pr + 1) * LANE] * jnp.exp(bcs_p)
            if dskip_ref is not None:
                yp = yp + xpair * dskip_ref[:, ls]
            for k in range(2):
                seg = bcs[k] - csT[h0 + k:h0 + k + 1, :]
                m = (scores * jnp.exp(jnp.where(mask, seg, -1e30))).astype(BF16)
                xk = jnp.where(lo_half if k == 0 else ~lo_half, xdt, 0.0).astype(BF16)
                yp = yp + jnp.dot(m, xk, preferred_element_type=F32)
            y_ref[:, ls] = yp.astype(y_ref.dtype)
            xdec.append((xdt * jnp.exp(tot_p - bcs_p)).astype(BF16))
            cdec_g.append(jnp.where(lo_row, col(cdec, h0), col(cdec, h0 + 1)))
        upd = lax.dot_general(bg, jnp.concatenate(xdec, axis=1), (((0,), (0,)), ((), ())),
                              preferred_element_type=F32)
        state_ref[:, gl] = st * jnp.concatenate(cdec_g, axis=1) + upd


def _ssd_kernel(xsf_ref, bcf_ref, dtf_ref, xsb_ref, bcb_ref, dtb_ref, alog_ref, dskip_ref, tri_ref,
                yf_ref, yb_ref, state_ref):
    @pl.when(pl.program_id(1) == 0)
    def _():
        state_ref[...] = jnp.zeros_like(state_ref)

    _ssd_chunk(xsf_ref, bcf_ref, dtf_ref, alog_ref.at[0], dskip_ref, tri_ref[0], yf_ref, state_ref.at[0])
    _ssd_chunk(xsb_ref, bcb_ref, dtb_ref, alog_ref.at[1], None, tri_ref[1], yb_ref, state_ref.at[1])


def _ssd(xs, bc, dt, a_log, ssd_d, batch, seqlen):
    R, W = xs.shape
    T = SSD_CHUNK
    c = seqlen // T
    alog = jnp.zeros((2, 1, LANE), F32).at[:, 0, :SSD_HEADS].set(a_log)
    dskip = jnp.repeat(ssd_d, SSD_HEAD_DIM).reshape(1, W)
    lower = np.tril(np.ones((T, T), np.float32))
    tri = jnp.asarray(np.stack([lower, lower.T]))
    fwd = lambda b, i: b * c + i
    bwd = lambda b, i: b * c + c - 1 - i
    row = lambda w, at, lane_blk=0: pl.BlockSpec((T, w), lambda b, i: (at(b, i), lane_blk))
    return pl.pallas_call(
        _ssd_kernel,
        grid=(batch, c),
        in_specs=[row(W, fwd), row(bc.shape[1], fwd), row(LANE, fwd, 0),
                  row(W, bwd), row(bc.shape[1], bwd), row(LANE, bwd, 1),
                  _const_spec(alog.shape), _const_spec(dskip.shape), _const_spec(tri.shape)],
        out_specs=[row(W, fwd), row(W, bwd)],
        out_shape=[jax.ShapeDtypeStruct((R, W), BF16)] * 2,
        scratch_shapes=[pltpu.VMEM((2, SSD_STATE, W), F32)],
        compiler_params=_params(("arbitrary", "arbitrary")),
        name="ssd",
    )(xs, bc, dt, xs, bc, dt, alog, dskip, tri)


def _filt_gen_kernel(fb_ref, w1h_ref, w1l_ref, b1_ref, w2h_ref, w2l_ref, b2_ref, w3h_ref, w3l_ref, b3_ref,
                     w4h_ref, w4l_ref, fr_ref, dl_ref, hs_ref, hd_ref, *, seqlen, cc):
    TF, C = hs_ref.shape
    L = seqlen
    n = pl.program_id(0) * TF + lax.broadcasted_iota(jnp.int32, (TF, 1), 0)
    pos = n.astype(F32)
    t = pos / (L - 1)
    lane = lax.broadcasted_iota(jnp.int32, (1, LANE), 1)
    bands = (HY_EMB - 1) // 2
    ang = fb_ref[...] * (2.0 * math.pi * pos / L)
    z = jnp.where(lane == 0, t,
                  jnp.where(lane <= bands, jnp.cos(ang),
                            jnp.where(lane <= 2 * bands, -jnp.sin(ang), 0.0)))
    fr = fr_ref[...]
    h = jnp.sin(fr * (_dot3(z, w1h_ref[...], w1l_ref[...]) + b1_ref[...]))
    h = jnp.sin(fr * (_dot3(h, w2h_ref[...], w2l_ref[...]) + b2_ref[...]))
    h = jnp.sin(fr * (_dot3(h, w3h_ref[...], w3l_ref[...]) + b3_ref[...]))
    for c0 in range(0, C, cc):
        win = jnp.exp(-t * dl_ref[:, c0:c0 + cc])
        hf = _dot3(h, w4h_ref[:, c0:c0 + cc], w4l_ref[:, c0:c0 + cc]) * win
        hb = _dot3(h, w4h_ref[:, C + c0:C + c0 + cc], w4l_ref[:, C + c0:C + c0 + cc]) * win
        hb = jnp.where(n == 0, 0.0, hb)
        hs_ref[:, c0:c0 + cc] = hf + hb
        hd_ref[:, c0:c0 + cc] = hf - hb


def _filt_gen(seqlen, w1, b1, w2, b2, w3, b3, w4, freq, tf=512, cc=512):
    L = seqlen
    hidden = w1.shape[1]
    C = w4.shape[1] // 2
    bands = (HY_EMB - 1) // 2
    fb = np.zeros((1, LANE), np.float64)
    fbv = np.linspace(1e-4, bands - 1, bands)
    fb[0, 1:1 + bands] = fbv
    fb[0, 1 + bands:1 + 2 * bands] = fbv
    max_decay = math.log(HY_DECAY_TARGET) / HY_FAST_DECAY
    min_decay = math.log(HY_DECAY_TARGET) / HY_SLOW_DECAY
    dl = np.abs(np.linspace(min_decay, max_decay, C)).reshape(1, C)
    w1p = jnp.zeros((LANE, hidden), F32).at[:HY_EMB].set(w1)
    row = lambda v: v.reshape(1, hidden)
    args = (jnp.asarray(fb, F32), *_split_bf16(w1p), row(b1), *_split_bf16(w2), row(b2),
            *_split_bf16(w3), row(b3), *_split_bf16(w4), row(freq), jnp.asarray(dl, F32))
    return pl.pallas_call(
        functools.partial(_filt_gen_kernel, seqlen=L, cc=cc),
        grid=(L // tf,),
        in_specs=[_const_spec(a.shape) for a in args],
        out_specs=[pl.BlockSpec((tf, C), lambda i: (i, 0))] * 2,
        out_shape=[jax.ShapeDtypeStruct((L, C), F32)] * 2,
        compiler_params=_params(("arbitrary",)),
        name="filt_gen",
    )(*args)


@functools.lru_cache(maxsize=None)
def _dft_tables(seqlen):
    N = 2 * seqlen
    N1 = DFT_N1
    N2 = N // N1
    K = N1 // 2
    k1 = np.arange(N1).reshape(1, N1, 1)
    n1 = np.arange(K).reshape(1, 1, K)
    n2 = np.arange(N2).reshape(N2, 1, 1)
    ang = -2.0 * np.pi * ((n2 * k1 + N2 * n1 * k1) % N) / N
    gr, gi = np.cos(ang), np.sin(ang)
    gk = np.concatenate([gr, gi], axis=1)
    ga = np.concatenate([np.concatenate([gr, -gi], axis=2),
                         np.concatenate([gi, gr], axis=2)], axis=1)
    grt, git = np.swapaxes(gr, 1, 2), np.swapaxes(gi, 1, 2)
    gc = np.concatenate([np.concatenate([grt, git], axis=2),
                         np.concatenate([-git, grt], axis=2)], axis=1)
    k2 = np.arange(N2).reshape(N2, 1)
    m2 = np.arange(N2).reshape(1, N2)
    ang2 = -2.0 * np.pi * ((k2 * m2) % N2) / N2
    fr, fi = np.cos(ang2), np.sin(ang2)
    fb = np.block([[fr, -fi], [fi, fr]])
    fbc = np.block([[fr, fi], [-fi, fr]])
    to = lambda x: x.astype(np.float32)
    return to(gk), to(ga), to(gc), to(fb), to(fbc)


def _mxu_tables(seqlen):
    return tuple(jnp.asarray(t).astype(BF16) for t in _dft_tables(seqlen))


def _stack_bf16(re, im):
    return jnp.concatenate([re, im], axis=0).astype(BF16)


def _filt_a_kernel(hs_ref, hd_ref, g_ref, o_ref):
    N1 = o_ref.shape[2]
    for r in range(SUBLANE):
        for q, h_ref in enumerate((hs_ref, hd_ref)):
            y = jnp.dot(g_ref[r], h_ref[:, r, :].astype(BF16), preferred_element_type=F32)
            o_ref[q, 0, :, r, :] = y[:N1]
            o_ref[q, 1, :, r, :] = y[N1:]


def _filt_b_kernel(a_ref, f_ref, o_ref, *, kb, C, cc, scale):
    N2 = a_ref.shape[3]
    for q in range(kb):
        for c0 in range(0, C, cc):
            cols = slice(c0, c0 + cc)
            for part in range(2):
                x = _stack_bf16(a_ref[part, 0, q, :, cols], a_ref[part, 1, q, :, cols])
                z = jnp.dot(f_ref[part * N2:(part + 1) * N2, :], x, preferred_element_type=F32)
                o_ref[part, q, :, cols] = z * scale


def _filter_spectrum(hs, hd, seqlen, kb, cb=512, cc=512):
    L, C = hs.shape
    N = 2 * L
    N1 = DFT_N1
    N2 = N // N1
    K = N1 // 2
    gk, _, _, fb, _ = _mxu_tables(seqlen)
    hspec = pl.BlockSpec((K, SUBLANE, cb), lambda j, c: (0, j, c))
    ka = pl.pallas_call(
        _filt_a_kernel,
        grid=(N2 // SUBLANE, C // cb),
        in_specs=[hspec, hspec, pl.BlockSpec((SUBLANE, 2 * N1, K), lambda j, c: (j, 0, 0))],
        out_specs=pl.BlockSpec((2, 2, N1, SUBLANE, cb), lambda j, c: (0, 0, 0, j, c)),
        out_shape=jax.ShapeDtypeStruct((2, 2, N1, N2, C), F32),
        compiler_params=_params(("arbitrary", "arbitrary")),
        name="filt_a",
    )(hs.reshape(K, N2, C), hd.reshape(K, N2, C), gk)
    return pl.pallas_call(
        functools.partial(_filt_b_kernel, kb=kb, C=C, cc=cc, scale=1.0 / N),
        grid=(N1 // kb,),
        in_specs=[pl.BlockSpec((2, 2, kb, N2, C), lambda j: (0, 0, j, 0, 0)), _const_spec(fb.shape)],
        out_specs=pl.BlockSpec((2, kb, N2, C), lambda j: (0, j, 0, 0)),
        out_shape=jax.ShapeDtypeStruct((2, N1, N2, C), F32),
        compiler_params=_params(("arbitrary",)),
        name="filt_b",
    )(ka, fb)


def _fft_a_kernel(x_ref, g_ref, o_ref):
    N1 = o_ref.shape[1]
    for r in range(SUBLANE):
        y = jnp.dot(g_ref[r], _stack_bf16(x_ref[0, :, r, :], x_ref[1, :, r, :]),
                    preferred_element_type=F32)
        o_ref[0, :, r, :] = y[:N1]
        o_ref[1, :, r, :] = y[N1:]


def _fft_b_kernel(a_ref, kh_ref, f_ref, fc_ref, o_ref, *, kb, C, cc):
    N2 = a_ref.shape[2]
    for q in range(kb):
        for c0 in range(0, C, cc):
            cols = slice(c0, c0 + cc)
            z = jnp.dot(f_ref[...], _stack_bf16(a_ref[0, q, :, cols], a_ref[1, q, :, cols]),
                        preferred_element_type=F32)
            zr, zi = z[:N2], z[N2:]
            kr, ki = kh_ref[0, q, :, cols], kh_ref[1, q, :, cols]
            v = jnp.dot(fc_ref[...], _stack_bf16(zr * kr - zi * ki, zr * ki + zi * kr),
                        preferred_element_type=F32)
            o_ref[0, q, :, cols] = v[:N2]
            o_ref[1, q, :, cols] = v[N2:]


def _fft_c_kernel(bt_ref, gc_ref, g_ref, x0_ref, bias_ref, nw_ref, avg_ref, o_ref, conv_ref):
    _, K, _, cb = o_ref.shape
    for r in range(SUBLANE):
        y = jnp.dot(gc_ref[r], _stack_bf16(bt_ref[0, :, r, :], bt_ref[1, :, r, :]),
                    preferred_element_type=F32)
        conv_ref[0, :, r, :] = y[:K]
        conv_ref[1, :, r, :] = y[K:]
    rows = 2 * K * SUBLANE
    flat = lambda ref: ref[...].reshape(rows, cb)
    gg = flat(g_ref)
    v = flat(x0_ref) * (flat(conv_ref) + gg * bias_ref[...])
    hi, lo = _split_bf16(v * v)
    ms = jnp.concatenate(
        [jnp.dot(hi[:, m:m + MXU_DIM], avg_ref[...], preferred_element_type=F32)
         + jnp.dot(lo[:, m:m + MXU_DIM], avg_ref[...], preferred_element_type=F32)
         for m in range(0, cb, MXU_DIM)], axis=1)
    o_ref[...] = (v * lax.rsqrt(ms + EPS) * nw_ref[...]).reshape(o_ref.shape)


def _long_conv(g, x0, kh, hy_bias, hy_norm_w, batch, seqlen, kb, cb=512, cc=512):
    R, C = g.shape
    L = seqlen
    N1 = DFT_N1
    N2 = 2 * L // N1
    K = N1 // 2
    bp = batch // 2
    _, ga, gc, fb, fbc = _mxu_tables(L)
    gv = g.reshape(bp, 2, K, N2, C)
    x0v = x0.reshape(bp, 2, K, N2, C)
    half = pl.BlockSpec((None, 2, K, SUBLANE, cb), lambda p, j, c: (p, 0, 0, j, c))
    full = pl.BlockSpec((None, 2, N1, SUBLANE, cb), lambda p, j, c: (p, 0, 0, j, c))
    grid_ac = (bp, N2 // SUBLANE, C // cb)
    a = pl.pallas_call(
        _fft_a_kernel,
        grid=grid_ac,
        in_specs=[half, pl.BlockSpec((SUBLANE, 2 * N1, N1), lambda p, j, c: (j, 0, 0))],
        out_specs=full,
        out_shape=jax.ShapeDtypeStruct((bp, 2, N1, N2, C), F32),
        compiler_params=_params(("arbitrary",) * 3),
        name="fft_a",
    )(gv, ga)
    bt = pl.pallas_call(
        functools.partial(_fft_b_kernel, kb=kb, C=C, cc=cc),
        grid=(N1 // kb, bp),
        in_specs=[pl.BlockSpec((None, 2, kb, N2, C), lambda j, p: (p, 0, j, 0, 0)),
                  pl.BlockSpec((2, kb, N2, C), lambda j, p: (0, j, 0, 0)),
                  _const_spec(fb.shape), _const_spec(fbc.shape)],
        out_specs=pl.BlockSpec((None, 2, kb, N2, C), lambda j, p: (p, 0, j, 0, 0)),
        out_shape=jax.ShapeDtypeStruct((bp, 2, N1, N2, C), F32),
        compiler_params=_params(("arbitrary", "arbitrary")),
        name="fft_b",
    )(a, kh, fb, fbc)
    avg = np.kron(np.eye(MXU_DIM // HY_GROUP), np.full((HY_GROUP, HY_GROUP), 1.0 / HY_GROUP))
    vec = pl.BlockSpec((1, cb), lambda p, j, c: (0, c))
    y = pl.pallas_call(
        _fft_c_kernel,
        grid=grid_ac,
        in_specs=[full, pl.BlockSpec((SUBLANE, N1, 2 * N1), lambda p, j, c: (j, 0, 0)),
                  half, half, vec, vec, _const_spec((MXU_DIM, MXU_DIM))],
        out_specs=half,
        out_shape=jax.ShapeDtypeStruct((bp, 2, K, N2, C), F32),
        scratch_shapes=[pltpu.VMEM((2, K, SUBLANE, cb), F32)],
        compiler_params=_params(("arbitrary",) * 3),
        name="fft_c",
    )(bt, gc, gv, x0v, hy_bias.reshape(1, C), hy_norm_w.reshape(1, C),
      jnp.asarray(avg.astype(np.float32)).astype(BF16))
    return y.reshape(R, C)


def _mix_rows(ys, z, yhy, h, p, snw_ref, wos_ref, woh_ref, nmp_ref, nfp_ref, wg_ref, wu_ref, wd_ref,
              nfo_ref, npp_ref, wpg_ref, wpp_ref, npo_ref):
    dot = lambda x, w_ref: jnp.dot(x.astype(BF16), w_ref[...], preferred_element_type=F32)
    ys = ys * _silu(z)
    gw = ys.shape[1] // SSD_GROUPS
    ysn = jnp.concatenate(
        [_rms(ys[:, k * gw:(k + 1) * gw], snw_ref[:, k * gw:(k + 1) * gw]) for k in range(SSD_GROUPS)],
        axis=1)
    mix = dot(ysn, wos_ref) + dot(yhy, woh_ref)
    h1 = h + _rms(mix, nmp_ref[...])
    u = _rms(h1, nfp_ref[...])
    ff = dot(_silu(dot(u, wg_ref)) * dot(u, wu_ref), wd_ref)
    h2 = h1 + _rms(ff, nfo_ref[...])
    gate = jax.nn.sigmoid(dot(_rms(h2, npp_ref[...]), wpg_ref))
    emb = dot(p, wpp_ref)
    return h2 + _rms(gate * emb, npo_ref[...])


def _mix_kernel(yf_ref, yb_ref, z_ref, yhy_ref, h_ref, p_ref, *rest):
    w_refs, o_ref = rest[:-1], rest[-1]
    ys = yf_ref[...].astype(F32) + yb_ref[...].astype(F32)
    o_ref[...] = _mix_rows(ys, z_ref[...].astype(F32), yhy_ref[...], h_ref[...], p_ref[...], *w_refs)


def _mix(yf, yb, z, yhy, h, p, consts, tm=256):
    R, D = h.shape
    row = lambda w: pl.BlockSpec((tm, w), lambda i: (i, 0))
    return pl.pallas_call(
        _mix_kernel,
        grid=(R // tm,),
        in_specs=[row(D), row(D), row(D), row(D), row(D), row(p.shape[1])]
        + [_const_spec(c.shape) for c in consts],
        out_specs=row(D),
        out_shape=jax.ShapeDtypeStruct((R, D), F32),
        compiler_params=_params(("arbitrary",)),
        name="mix",
    )(yf, yb, z, yhy, h, p, *consts)


def _layer(x, p, w):
    batch, seqlen, D = x.shape
    R = batch * seqlen
    x2d = x.reshape(R, D)
    row = lambda v: v.reshape(1, -1)
    o1 = SSD_HEADS * SSD_HEAD_DIM
    o2 = o1 + w["ssd_conv_w"].shape[1]
    o3 = o2 + 2 * SSD_HEADS
    w_in = w["w_in"]
    H = SSD_HEADS
    wdt = jnp.zeros((D, 2 * LANE), F32)
    wdt = wdt.at[:, :H].set(w_in[:, o2:o2 + H]).at[:, LANE:LANE + H].set(w_in[:, o2 + H:o3])
    dtb = jnp.zeros((1, 2 * LANE), F32)
    dtb = dtb.at[0, :H].set(w["ssd_dt_bias"][0]).at[0, LANE:LANE + H].set(w["ssd_dt_bias"][1])
    z, dt, xs, bc, g, x0 = _in_proj(
        x2d, row(w["norm_mix_pre"]), w_in[:, :o1].astype(BF16), w_in[:, o1:o2].astype(BF16),
        w_in[:, o3:].astype(BF16), wdt, dtb, w["ssd_conv_w"], w["ssd_conv_b"], w["hy_conv_w"],
        w["hy_conv_b"], batch, seqlen)
    yf, yb = _ssd(xs, bc, dt, w["ssd_a_log"], w["ssd_d"], batch, seqlen)

    n2 = 2 * seqlen // DFT_N1
    kb = max(1, min(DFT_N1, 256 // n2))
    hs, hd = _filt_gen(seqlen, w["hy_f_w1"], w["hy_f_b1"], w["hy_f_w2"], w["hy_f_b2"], w["hy_f_w3"],
                       w["hy_f_b3"], w["hy_f_w4"], w["hy_f_freq"], tf=min(512, seqlen))
    kh = _filter_spectrum(hs, hd, seqlen, kb)
    yhy = _long_conv(g, x0, kh, w["hy_bias"], w["hy_norm_w"], batch, seqlen, kb)

    w_out = w["w_out"].astype(BF16)
    consts = [row(w["ssd_norm_w"]), w_out[:o1], w_out[o1:], row(w["norm_mix_post"]),
              row(w["norm_ffn_pre"]), w["w_gate"].astype(BF16), w["w_up"].astype(BF16),
              w["w_down"].astype(BF16), row(w["norm_ffn_post"]), row(w["ple_norm_pre"]),
              w["w_ple_gate"].astype(BF16), w["w_ple_proj"].astype(BF16), row(w["ple_norm_post"])]
    out = _mix(yf, yb, z, yhy, x2d, p.reshape(R, -1), consts)
    return out.reshape(batch, seqlen, D)


_WEIGHT_NAMES = (
    "norm_mix_pre", "w_in", "ssd_conv_w", "ssd_conv_b", "ssd_dt_bias", "ssd_a_log", "ssd_d",
    "ssd_norm_w", "hy_conv_w", "hy_conv_b", "hy_f_w1", "hy_f_b1", "hy_f_w2", "hy_f_b2", "hy_f_w3",
    "hy_f_b3", "hy_f_w4", "hy_f_freq", "hy_bias", "hy_norm_w", "w_out", "norm_mix_post",
    "norm_ffn_pre", "w_gate", "w_up", "w_down", "norm_ffn_post", "ple_norm_pre", "w_ple_gate",
    "w_ple_proj", "ple_norm_post")


def _trunk(x, p, weights):
    h = x
    for i in range(p.shape[0]):
        h = _layer(h, p[i], {n: v[i] for n, v in zip(_WEIGHT_NAMES, weights)})
    return h


def kernel(x_prompt, x_sample, p_prompt, p_sample, norm_mix_pre, w_in, ssd_conv_w, ssd_conv_b, ssd_dt_bias, ssd_a_log, ssd_d, ssd_norm_w, hy_conv_w, hy_conv_b, hy_f_w1, hy_f_b1, hy_f_w2, hy_f_b2, hy_f_w3, hy_f_b3, hy_f_w4, hy_f_freq, hy_bias, hy_norm_w, w_out, norm_mix_post, norm_ffn_pre, w_gate, w_up, w_down, norm_ffn_post, ple_norm_pre, w_ple_gate, w_ple_proj, ple_norm_post):
    weights = (norm_mix_pre, w_in, ssd_conv_w, ssd_conv_b, ssd_dt_bias, ssd_a_log, ssd_d, ssd_norm_w,
               hy_conv_w, hy_conv_b, hy_f_w1, hy_f_b1, hy_f_w2, hy_f_b2, hy_f_w3, hy_f_b3, hy_f_w4,
               hy_f_freq, hy_bias, hy_norm_w, w_out, norm_mix_post, norm_ffn_pre, w_gate, w_up,
               w_down, norm_ffn_post, ple_norm_pre, w_ple_gate, w_ple_proj, ple_norm_post)
    return (_trunk(x_prompt, p_prompt, weights), _trunk(x_sample, p_sample, weights))
```

```python
import functools
import math

import numpy as np
import jax
import jax.numpy as jnp
from jax import lax
from jax.experimental import pallas as pl
from jax.experimental.pallas import tpu as pltpu

F32 = jnp.float32
BF16 = jnp.bfloat16
HIGHEST = lax.Precision.HIGHEST

EPS = 1e-6
LANE = 128
SUBLANE = 8
MXU_DIM = 256
VMEM_LIMIT = 56 * 1024 * 1024

SSD_HEAD_DIM = 64
SSD_HEADS = 16
SSD_GROUPS = 2
SSD_STATE = 128
SSD_CHUNK = 128
HY_GROUP = 64
HY_EMB = 33
HY_FAST_DECAY = 0.3
HY_SLOW_DECAY = 1.5
HY_DECAY_TARGET = 1e-2
DFT_N1 = 128
HALO = 16
ROWG = 16


def _params(sem):
    return pltpu.CompilerParams(dimension_semantics=sem, vmem_limit_bytes=VMEM_LIMIT)


def _const_spec(shape):
    nd = len(shape)
    return pl.BlockSpec(shape, lambda *_: (0,) * nd, pipeline_mode=pl.Buffered(1))


def _rms(x, w):
    return x * lax.rsqrt(jnp.mean(x * x, axis=-1, keepdims=True) + EPS) * w


def _silu(x):
    return x * jax.nn.sigmoid(x)


def _softplus(x):
    return jnp.maximum(x, 0.0) + jnp.log1p(jnp.exp(-jnp.abs(x)))


def _split_bf16(x):
    hi = x.astype(BF16)
    return hi, (x - hi.astype(F32)).astype(BF16)


def _dot3(x, w_hi, w_lo):
    x_hi, x_lo = _split_bf16(x)
    dot = lambda a, b: jnp.dot(a, b, preferred_element_type=F32)
    return dot(x_hi, w_hi) + (dot(x_hi, w_lo) + dot(x_lo, w_hi))


def _shifted(xg, o):
    if o == 0:
        return xg
    r = pltpu.roll(xg, (-o) % SUBLANE, axis=1)
    sub = lax.broadcasted_iota(jnp.int32, (1, SUBLANE, 1), 1)
    if o < 0:
        return jnp.where(sub < -o, jnp.concatenate([r[:1], r[:-1]], axis=0), r)
    return jnp.where(sub >= SUBLANE - o, jnp.concatenate([r[1:], r[-1:]], axis=0), r)


def _dwconv(p_ext, cw_ref, cb_ref, cols, width):
    rows, W = p_ext.shape
    xg = p_ext.reshape(rows // SUBLANE, SUBLANE, W)
    acc = cb_ref[:, cols].reshape(1, 1, W)
    for j in range(width):
        acc = acc + cw_ref[j:j + 1, cols].reshape(1, 1, W) * _shifted(xg, j - width // 2)
    hg = HALO // SUBLANE
    return acc[hg:-hg].reshape(rows - 2 * HALO, W)


def _in_proj_kernel(xm_ref, xp_ref, xn_ref, nw_ref, wz_ref, wxbc_ref, why_ref, wdth_ref, wdtl_ref, dtb_ref,
                    scw_ref, scb_ref, hcw_ref, hcb_ref, z_ref, dt_ref, xs_ref, bc_ref, g_ref, x0_ref,
                    *, ntiles, ssd_w, hy_w, cc):
    i = pl.program_id(1)
    tm = xm_ref.shape[0]
    W = xs_ref.shape[1]
    C = g_ref.shape[1]
    xp = jnp.where(i == 0, 0.0, xp_ref[...])
    xn = jnp.where(i == ntiles - 1, 0.0, xn_ref[...])
    u = _rms(jnp.concatenate([xp, xm_ref[...], xn], axis=0), nw_ref[...])
    ub, ul = _split_bf16(u)
    ubm, ulm = ub[HALO:HALO + tm], ul[HALO:HALO + tm]
    dot = lambda a, b: jnp.dot(a, b, preferred_element_type=F32)
    for j in range(0, W, cc):
        z_ref[:, j:j + cc] = dot(ubm, wz_ref[:, j:j + cc]).astype(z_ref.dtype)
    dt_raw = dot(ubm, wdth_ref[...]) + (dot(ubm, wdtl_ref[...]) + dot(ulm, wdth_ref[...]))
    dt_ref[...] = _softplus(dt_raw + dtb_ref[...])
    for c0 in range(0, wxbc_ref.shape[1], cc):
        cols = slice(c0, c0 + cc)
        act = _silu(_dwconv(dot(ub, wxbc_ref[:, cols]), scw_ref, scb_ref, cols, ssd_w))
        if c0 < W:
            xs_ref[:, cols] = act
        else:
            bc_ref[:, c0 - W:c0 - W + cc] = act.astype(BF16)
    for c0 in range(0, C, cc):
        conv = lambda off: _dwconv(dot(ub, why_ref[:, off + c0:off + c0 + cc]), hcw_ref, hcb_ref,
                                   slice(off + c0, off + c0 + cc), hy_w)
        x0_ref[:, c0:c0 + cc] = conv(0)
        g_ref[:, c0:c0 + cc] = conv(2 * C) * conv(C)


def _in_proj(x2d, nw, wz, wxbc, why, wdt, dt_bias, ssd_cw, ssd_cb, hy_cw, hy_cb, batch, seqlen, tm=512, cc=512):
    R, D = x2d.shape
    W = wz.shape[1]
    XW = wxbc.shape[1]
    HW = why.shape[1]
    C = HW // 3
    wdth, wdtl = _split_bf16(wdt)
    pad_w = lambda w: jnp.zeros((SUBLANE, w.shape[1]), F32).at[:w.shape[0]].set(w)
    nt = seqlen // tm
    hb = tm // HALO
    nhalo = R // HALO
    blk = lambda b, i: b * nt + i
    main = lambda w: pl.BlockSpec((tm, w), lambda b, i: (blk(b, i), 0))
    consts = (nw, wz, wxbc, why, wdth, wdtl, dt_bias, pad_w(ssd_cw), ssd_cb.reshape(1, XW),
              pad_w(hy_cw), hy_cb.reshape(1, HW))
    kern = functools.partial(_in_proj_kernel, ntiles=nt, ssd_w=ssd_cw.shape[0], hy_w=hy_cw.shape[0], cc=cc)
    return pl.pallas_call(
        kern,
        grid=(batch, nt),
        in_specs=[main(D),
                  pl.BlockSpec((HALO, D), lambda b, i: (jnp.maximum(blk(b, i) * hb - 1, 0), 0)),
                  pl.BlockSpec((HALO, D), lambda b, i: (jnp.minimum((blk(b, i) + 1) * hb, nhalo - 1), 0))]
        + [_const_spec(a.shape) for a in consts],
        out_specs=[main(W), main(wdt.shape[1]), main(W), main(XW - W), main(C), main(C)],
        out_shape=[jax.ShapeDtypeStruct((R, W), BF16), jax.ShapeDtypeStruct((R, wdt.shape[1]), F32),
                   jax.ShapeDtypeStruct((R, W), F32), jax.ShapeDtypeStruct((R, XW - W), BF16),
                   jax.ShapeDtypeStruct((R, C), F32), jax.ShapeDtypeStruct((R, C), F32)],
        compiler_params=_params(("arbitrary", "arbitrary")),
        name="in_proj",
    )(x2d, x2d, x2d, *consts)


def _ssd_chunk(xs_ref, bc_ref, dt_ref, alog_ref, dskip_ref, tri, y_ref, state_ref):
    T = xs_ref.shape[0]
    W = SSD_HEADS * SSD_HEAD_DIM
    GN = SSD_GROUPS * SSD_STATE
    GW = W // SSD_GROUPS
    dtv = dt_ref[...]
    a = dtv * (-jnp.exp(alog_ref[...]))
    mask = tri > 0.5
    cs = jnp.dot(tri, a, precision=HIGHEST, preferred_element_type=F32)
    tot = jnp.sum(a, axis=0, keepdims=True)
    cdec = jnp.exp(tot)
    csT = cs.T
    lo_half = lax.broadcasted_iota(jnp.int32, (T, LANE), 1) < SSD_HEAD_DIM
    lo_row = lo_half[0:1]
    col = lambda v, h: jnp.broadcast_to(v[:, h:h + 1], (v.shape[0], LANE))

    heads_per_group = SSD_HEADS // SSD_GROUPS
    for g in range(SSD_GROUPS):
        cg = bc_ref[:, GN + g * SSD_STATE:GN + (g + 1) * SSD_STATE]
        bg = bc_ref[:, g * SSD_STATE:(g + 1) * SSD_STATE]
        gl = slice(g * GW, (g + 1) * GW)
        scores = lax.dot_general(cg, bg, (((1,), (1,)), ((), ())), preferred_element_type=F32)
        st = state_ref[:, gl]
        yoff = jnp.dot(cg, st.astype(BF16), preferred_element_type=F32)
        xdec = []
        cdec_g = []
        for pr in range(heads_per_group // 2):
            h0 = g * heads_per_group + 2 * pr
            ls = slice(h0 * SSD_HEAD_DIM, h0 * SSD_HEAD_DIM + LANE)
            xpair = xs_ref[:, ls]
            bcs = [col(cs, h0), col(cs, h0 + 1)]
            bcs_p = jnp.where(lo_half, bcs[0], bcs[1])
            pidx = jnp.where(lo_half, h0, h0 + 1)
            xdt = xpair * jnp.take_along_axis(dtv, pidx, axis=1)
            tot_p = jnp.where(lo_row, col(tot, h0), col(tot, h0 + 1))
            yp = yoff[:, pr * LANE:(pr + 1) * LANE] * jnp.exp(bcs_p)
            if dskip_ref is not None:
                yp = yp + xpair * dskip_ref[:, ls]
            for k in range(2):
                seg = bcs[k] - csT[h0 + k:h0 + k + 1, :]
                m = (scores * jnp.exp(jnp.where(mask, seg, -1e30))).astype(BF16)
                xk = jnp.where(lo_half if k == 0 else ~lo_half, xdt, 0.0).astype(BF16)
                yp = yp + jnp.dot(m, xk, preferred_element_type=F32)
            y_ref[:, ls] = yp.astype(y_ref.dtype)
            xdec.append((xdt * jnp.exp(tot_p - bcs_p)).astype(BF16))
            cdec_g.append(jnp.where(lo_row, col(cdec, h0), col(cdec, h0 + 1)))
        upd = lax.dot_general(bg, jnp.concatenate(xdec, axis=1), (((0,), (0,)), ((), ())),
                              preferred_element_type=F32)
        state_ref[:, gl] = st * jnp.concatenate(cdec_g, axis=1) + upd


def _ssd_kernel(xsf_ref, bcf_ref, dtf_ref, xsb_ref, bcb_ref, dtb_ref, alog_ref, dskip_ref, tri_ref,
                yf_ref, yb_ref, state_ref):
    @pl.when(pl.program_id(1) == 0)
    def _():
        state_ref[...] = jnp.zeros_like(state_ref)

    _ssd_chunk(xsf_ref, bcf_ref, dtf_ref, alog_ref.at[0], dskip_ref, tri_ref[0], yf_ref, state_ref.at[0])
    _ssd_chunk(xsb_ref, bcb_ref, dtb_ref, alog_ref.at[1], None, tri_ref[1], yb_ref, state_ref.at[1])


def _ssd(xs, bc, dt, a_log, ssd_d, batch, seqlen):
    R, W = xs.shape
    T = SSD_CHUNK
    c = seqlen // T
    alog = jnp.zeros((2, 1, LANE), F32).at[:, 0, :SSD_HEADS].set(a_log)
    dskip = jnp.repeat(ssd_d, SSD_HEAD_DIM).reshape(1, W)
    lower = np.tril(np.ones((T, T), np.float32))
    tri = jnp.asarray(np.stack([lower, lower.T]))
    fwd = lambda b, i: b * c + i
    bwd = lambda b, i: b * c + c - 1 - i
    row = lambda w, at, lane_blk=0: pl.BlockSpec((T, w), lambda b, i: (at(b, i), lane_blk))
    return pl.pallas_call(
        _ssd_kernel,
        grid=(batch, c),
        in_specs=[row(W, fwd), row(bc.shape[1], fwd), row(LANE, fwd, 0),
                  row(W, bwd), row(bc.shape[1], bwd), row(LANE, bwd, 1),
                  _const_spec(alog.shape), _const_spec(dskip.shape), _const_spec(tri.shape)],
        out_specs=[row(W, fwd), row(W, bwd)],
        out_shape=[jax.ShapeDtypeStruct((R, W), BF16)] * 2,
        scratch_shapes=[pltpu.VMEM((2, SSD_STATE, W), F32)],
        compiler_params=_params(("arbitrary", "arbitrary")),
        name="ssd",
    )(xs, bc, dt, xs, bc, dt, alog, dskip, tri)


@functools.lru_cache(maxsize=None)
def _dft_tables(seqlen):
    N = 2 * seqlen
    N1 = DFT_N1
    N2 = N // N1
    K = N1 // 2
    k1 = np.arange(N1).reshape(1, N1, 1)
    n1 = np.arange(K).reshape(1, 1, K)
    n2 = np.arange(N2).reshape(N2, 1, 1)
    ang = -2.0 * np.pi * ((n2 * k1 + N2 * n1 * k1) % N) / N
    gr, gi = np.cos(ang), np.sin(ang)
    gk = np.concatenate([gr, gi], axis=1)
    ga = np.concatenate([np.concatenate([gr, -gi], axis=2),
                         np.concatenate([gi, gr], axis=2)], axis=1)
    grt, git = np.swapaxes(gr, 1, 2), np.swapaxes(gi, 1, 2)
    gc = np.concatenate([np.concatenate([grt, git], axis=2),
                         np.concatenate([-git, grt], axis=2)], axis=1)
    k2 = np.arange(N2).reshape(N2, 1)
    m2 = np.arange(N2).reshape(1, N2)
    ang2 = -2.0 * np.pi * ((k2 * m2) % N2) / N2
    fr, fi = np.cos(ang2), np.sin(ang2)
    fb = np.block([[fr, -fi], [fi, fr]])
    fbc = np.block([[fr, fi], [-fi, fr]])
    to = lambda x: x.astype(np.float32)
    return to(gk), to(ga), to(gc), to(fb), to(fbc)


def _mxu_tables(seqlen):
    return tuple(jnp.asarray(t).astype(BF16) for t in _dft_tables(seqlen))


def _stack_bf16(re, im):
    return jnp.concatenate([re, im], axis=0).astype(BF16)


def _rows_major(x):
    return pltpu.einshape("krc->rkc", x)


def _rows_minor(parts):
    return pltpu.einshape("rkc->krc", jnp.stack(parts, axis=0))


def _filt_a_kernel(fb_ref, w1h_ref, w1l_ref, b1_ref, w2h_ref, w2l_ref, b2_ref, w3h_ref, w3l_ref, b3_ref,
                   fr_ref, w4fh_ref, w4fl_ref, w4bh_ref, w4bl_ref, dl_ref, gk_ref, o_ref, h_ref, *, seqlen):
    L = seqlen
    N1 = o_ref.shape[2]
    K = N1 // 2
    N2 = 2 * L // N1
    rows = ROWG * K
    ridx = lax.broadcasted_iota(jnp.int32, (rows, 1), 0)
    n = N2 * (ridx % K) + (pl.program_id(0) * ROWG + ridx // K)
    pos = n.astype(F32)
    t = pos / (L - 1)

    @pl.when(pl.program_id(1) == 0)
    def _():
        lane = lax.broadcasted_iota(jnp.int32, (1, LANE), 1)
        bands = (HY_EMB - 1) // 2
        ang = fb_ref[...] * (2.0 * math.pi * pos / L)
        z = jnp.where(lane == 0, t,
                      jnp.where(lane <= bands, jnp.cos(ang),
                                jnp.where(lane <= 2 * bands, -jnp.sin(ang), 0.0)))
        fr = fr_ref[...]
        h = jnp.sin(fr * (_dot3(z, w1h_ref[...], w1l_ref[...]) + b1_ref[...]))
        h = jnp.sin(fr * (_dot3(h, w2h_ref[...], w2l_ref[...]) + b2_ref[...]))
        h_ref[...] = jnp.sin(fr * (_dot3(h, w3h_ref[...], w3l_ref[...]) + b3_ref[...]))

    h = h_ref[...]
    win = jnp.exp(-t * dl_ref[...])
    hf = _dot3(h, w4fh_ref[...], w4fl_ref[...]) * win
    hb = _dot3(h, w4bh_ref[...], w4bl_ref[...]) * win
    hb = jnp.where(n == 0, 0.0, hb)
    parts = ((hf + hb).astype(BF16), (hf - hb).astype(BF16))
    for q in range(2):
        ys = [jnp.dot(gk_ref[r], parts[q][r * K:(r + 1) * K], preferred_element_type=F32) for r in range(ROWG)]
        for part in range(2):
            o_ref[q, part] = _rows_minor([y[part * N1:(part + 1) * N1].astype(BF16) for y in ys])


def _filt_b_kernel(a_ref, f_ref, o_ref, *, kb, C, cc, scale):
    N2 = a_ref.shape[3]
    for q in range(kb):
        for c0 in range(0, C, cc):
            cols = slice(c0, c0 + cc)
            for part in range(2):
                x = _stack_bf16(a_ref[part, 0, q, :, cols], a_ref[part, 1, q, :, cols])
                z = jnp.dot(f_ref[part * N2:(part + 1) * N2, :], x, preferred_element_type=F32)
                o_ref[part, q, :, cols] = (z * scale).astype(o_ref.dtype)


def _filter_spectrum(seqlen, w1, b1, w2, b2, w3, b3, w4, freq, kb, cb=256, cc=512):
    L = seqlen
    hidden = w1.shape[1]
    C = w4.shape[1] // 2
    N = 2 * L
    N1 = DFT_N1
    N2 = N // N1
    K = N1 // 2
    ncb = C // cb
    gk, _, _, fb, _ = _mxu_tables(seqlen)
    bands = (HY_EMB - 1) // 2
    fbands = np.zeros((1, LANE), np.float64)
    fbv = np.linspace(1e-4, bands - 1, bands)
    fbands[0, 1:1 + bands] = fbv
    fbands[0, 1 + bands:1 + 2 * bands] = fbv
    max_decay = math.log(HY_DECAY_TARGET) / HY_FAST_DECAY
    min_decay = math.log(HY_DECAY_TARGET) / HY_SLOW_DECAY
    dl = np.abs(np.linspace(min_decay, max_decay, C)).reshape(1, C)
    w1p = jnp.zeros((LANE, hidden), F32).at[:HY_EMB].set(w1)
    row = lambda v: v.reshape(1, hidden)
    w4h, w4l = _split_bf16(w4)
    mlp = (jnp.asarray(fbands, F32), *_split_bf16(w1p), row(b1), *_split_bf16(w2), row(b2),
           *_split_bf16(w3), row(b3), row(freq))
    w4f = pl.BlockSpec((hidden, cb), lambda j, c: (0, c))
    w4b = pl.BlockSpec((hidden, cb), lambda j, c: (0, ncb + c))
    ka = pl.pallas_call(
        functools.partial(_filt_a_kernel, seqlen=L),
        grid=(N2 // ROWG, ncb),
        in_specs=[_const_spec(a.shape) for a in mlp]
        + [w4f, w4f, w4b, w4b, pl.BlockSpec((1, cb), lambda j, c: (0, c)),
           pl.BlockSpec((ROWG, 2 * N1, K), lambda j, c: (j, 0, 0))],
        out_specs=pl.BlockSpec((2, 2, N1, ROWG, cb), lambda j, c: (0, 0, 0, j, c)),
        out_shape=jax.ShapeDtypeStruct((2, 2, N1, N2, C), BF16),
        scratch_shapes=[pltpu.VMEM((ROWG * K, hidden), F32)],
        compiler_params=_params(("arbitrary", "arbitrary")),
        name="filt_a",
    )(*mlp, w4h, w4l, w4h, w4l, jnp.asarray(dl, F32), gk)
    return pl.pallas_call(
        functools.partial(_filt_b_kernel, kb=kb, C=C, cc=cc, scale=1.0 / N),
        grid=(N1 // kb,),
        in_specs=[pl.BlockSpec((2, 2, kb, N2, C), lambda j: (0, 0, j, 0, 0)), _const_spec(fb.shape)],
        out_specs=pl.BlockSpec((2, kb, N2, C), lambda j: (0, j, 0, 0)),
        out_shape=jax.ShapeDtypeStruct((2, N1, N2, C), BF16),
        compiler_params=_params(("arbitrary",)),
        name="filt_b",
    )(ka, fb)


def _fft_a_kernel(x_ref, g_ref, o_ref):
    N1 = o_ref.shape[1]
    xt = [_rows_major(x_ref[part].astype(BF16)) for part in range(2)]
    ys = [jnp.dot(g_ref[r], jnp.concatenate([xt[0][r], xt[1][r]], axis=0), preferred_element_type=F32)
          for r in range(ROWG)]
    for part in range(2):
        o_ref[part] = _rows_minor([y[part * N1:(part + 1) * N1].astype(BF16) for y in ys])


def _fft_b_kernel(a_ref, kh_ref, f_ref, fc_ref, o_ref, *, kb):
    N2 = a_ref.shape[2]
    dot = lambda m, x: jnp.dot(m, x, preferred_element_type=F32)
    zs = [dot(f_ref[...], jnp.concatenate([a_ref[0, q], a_ref[1, q]], axis=0)) for q in range(kb)]
    ws = []
    for q, z in enumerate(zs):
        zr, zi = z[:N2], z[N2:]
        kr, ki = kh_ref[0, q].astype(F32), kh_ref[1, q].astype(F32)
        ws.append(_stack_bf16(zr * kr - zi * ki, zr * ki + zi * kr))
    for q, w in enumerate(ws):
        v = dot(fc_ref[...], w)
        o_ref[0, q] = v[:N2].astype(o_ref.dtype)
        o_ref[1, q] = v[N2:].astype(o_ref.dtype)


def _fft_c_kernel(bt_ref, gc_ref, g_ref, x0_ref, bias_ref, nw_ref, avg_ref, o_ref):
    _, K, _, cb = o_ref.shape
    bt = [_rows_major(bt_ref[part]) for part in range(2)]
    ys = [jnp.dot(gc_ref[r], jnp.concatenate([bt[0][r], bt[1][r]], axis=0), preferred_element_type=F32)
          for r in range(ROWG)]
    rows = 2 * K * ROWG
    conv = jnp.stack([_rows_minor([y[s * K:(s + 1) * K] for y in ys]) for s in range(2)], axis=0)
    flat = lambda x: x.reshape(rows, cb)
    v = flat(x0_ref[...]) * (flat(conv) + flat(g_ref[...]) * bias_ref[...])
    hi, lo = _split_bf16(v * v)
    ms = jnp.concatenate(
        [jnp.dot(hi[:, m:m + MXU_DIM], avg_ref[...], preferred_element_type=F32)
         + jnp.dot(lo[:, m:m + MXU_DIM], avg_ref[...], preferred_element_type=F32)
         for m in range(0, cb, MXU_DIM)], axis=1)
    o_ref[...] = (v * lax.rsqrt(ms + EPS) * nw_ref[...]).reshape(o_ref.shape)


def _long_conv(g, x0, kh, hy_bias, hy_norm_w, batch, seqlen, kb, cb=256):
    R, C = g.shape
    L = seqlen
    N1 = DFT_N1
    N2 = 2 * L // N1
    K = N1 // 2
    bp = batch // 2
    _, ga, gc, fb, fbc = _mxu_tables(L)
    gv = g.reshape(bp, 2, K, N2, C)
    x0v = x0.reshape(bp, 2, K, N2, C)
    half = pl.BlockSpec((None, 2, K, ROWG, cb), lambda p, j, c: (p, 0, 0, j, c))
    full = pl.BlockSpec((None, 2, N1, ROWG, cb), lambda p, j, c: (p, 0, 0, j, c))
    grid_ac = (bp, N2 // ROWG, C // cb)
    a = pl.pallas_call(
        _fft_a_kernel,
        grid=grid_ac,
        in_specs=[half, pl.BlockSpec((ROWG, 2 * N1, N1), lambda p, j, c: (j, 0, 0))],
        out_specs=full,
        out_shape=jax.ShapeDtypeStruct((bp, 2, N1, N2, C), BF16),
        compiler_params=_params(("arbitrary",) * 3),
        name="fft_a",
    )(gv, ga)
    bt = pl.pallas_call(
        functools.partial(_fft_b_kernel, kb=kb),
        grid=(N1 // kb, bp),
        in_specs=[pl.BlockSpec((None, 2, kb, N2, C), lambda j, p: (p, 0, j, 0, 0)),
                  pl.BlockSpec((2, kb, N2, C), lambda j, p: (0, j, 0, 0)),
                  _const_spec(fb.shape), _const_spec(fbc.shape)],
        out_specs=pl.BlockSpec((None, 2, kb, N2, C), lambda j, p: (p, 0, j, 0, 0)),
        out_shape=jax.ShapeDtypeStruct((bp, 2, N1, N2, C), BF16),
        compiler_params=_params(("arbitrary", "arbitrary")),
        name="fft_b",
    )(a, kh, fb, fbc)
    avg = np.kron(np.eye(MXU_DIM // HY_GROUP), np.full((HY_GROUP, HY_GROUP), 1.0 / HY_GROUP))
    vec = pl.BlockSpec((1, cb), lambda p, j, c: (0, c))
    y = pl.pallas_call(
        _fft_c_kernel,
        grid=grid_ac,
        in_specs=[full, pl.BlockSpec((ROWG, N1, 2 * N1), lambda p, j, c: (j, 0, 0)),
                  half, half, vec, vec, _const_spec((MXU_DIM, MXU_DIM))],
        out_specs=half,
        out_shape=jax.ShapeDtypeStruct((bp, 2, K, N2, C), F32),
        compiler_params=_params(("arbitrary",) * 3),
        name="fft_c",
    )(bt, gc, gv, x0v, hy_bias.reshape(1, C), hy_norm_w.reshape(1, C),
      jnp.asarray(avg.astype(np.float32)).astype(BF16))
    return y.reshape(R, C)


def _mix_rows(ys, z, yhy, h, p, snw_ref, wos_ref, woh_ref, nmp_ref, nfp_ref, wg_ref, wu_ref, wd_ref,
              nfo_ref, npp_ref, wpg_ref, wpp_ref, npo_ref):
    dot = lambda x, w_ref: jnp.dot(x.astype(BF16), w_ref[...], preferred_element_type=F32)
    ys = ys * _silu(z)
    gw = ys.shape[1] // SSD_GROUPS
    ysn = jnp.concatenate(
        [_rms(ys[:, k * gw:(k + 1) * gw], snw_ref[:, k * gw:(k + 1) * gw]) for k in range(SSD_GROUPS)],
        axis=1)
    mix = dot(ysn, wos_ref) + dot(yhy, woh_ref)
    h1 = h + _rms(mix, nmp_ref[...])
    u = _rms(h1, nfp_ref[...])
    ff = dot(_silu(dot(u, wg_ref)) * dot(u, wu_ref), wd_ref)
    h2 = h1 + _rms(ff, nfo_ref[...])
    gate = jax.nn.sigmoid(dot(_rms(h2, npp_ref[...]), wpg_ref))
    emb = dot(p, wpp_ref)
    return h2 + _rms(gate * emb, npo_ref[...])


def _mix_kernel(yf_ref, yb_ref, z_ref, yhy_ref, h_ref, p_ref, *rest):
    w_refs, o_ref = rest[:-1], rest[-1]
    ys = yf_ref[...].astype(F32) + yb_ref[...].astype(F32)
    o_ref[...] = _mix_rows(ys, z_ref[...].astype(F32), yhy_ref[...], h_ref[...], p_ref[...], *w_refs)


def _mix(yf, yb, z, yhy, h, p, consts, tm=256):
    R, D = h.shape
    row = lambda w: pl.BlockSpec((tm, w), lambda i: (i, 0))
    return pl.pallas_call(
        _mix_kernel,
        grid=(R // tm,),
        in_specs=[row(D), row(D), row(D), row(D), row(D), row(p.shape[1])]
        + [_const_spec(c.shape) for c in consts],
        out_specs=row(D),
        out_shape=jax.ShapeDtypeStruct((R, D), F32),
        compiler_params=_params(("arbitrary",)),
        name="mix",
    )(yf, yb, z, yhy, h, p, *consts)


def _layer(x, p, w):
    batch, seqlen, D = x.shape
    R = batch * seqlen
    x2d = x.reshape(R, D)
    row = lambda v: v.reshape(1, -1)
    o1 = SSD_HEADS * SSD_HEAD_DIM
    o2 = o1 + w["ssd_conv_w"].shape[1]
    o3 = o2 + 2 * SSD_HEADS
    w_in = w["w_in"]
    H = SSD_HEADS
    wdt = jnp.zeros((D, 2 * LANE), F32)
    wdt = wdt.at[:, :H].set(w_in[:, o2:o2 + H]).at[:, LANE:LANE + H].set(w_in[:, o2 + H:o3])
    dtb = jnp.zeros((1, 2 * LANE), F32)
    dtb = dtb.at[0, :H].set(w["ssd_dt_bias"][0]).at[0, LANE:LANE + H].set(w["ssd_dt_bias"][1])
    z, dt, xs, bc, g, x0 = _in_proj(
        x2d, row(w["norm_mix_pre"]), w_in[:, :o1].astype(BF16), w_in[:, o1:o2].astype(BF16),
        w_in[:, o3:].astype(BF16), wdt, dtb, w["ssd_conv_w"], w["ssd_conv_b"], w["hy_conv_w"],
        w["hy_conv_b"], batch, seqlen)
    yf, yb = _ssd(xs, bc, dt, w["ssd_a_log"], w["ssd_d"], batch, seqlen)

    n2 = 2 * seqlen // DFT_N1
    kb = max(1, min(DFT_N1, 256 // n2))
    kh = _filter_spectrum(seqlen, w["hy_f_w1"], w["hy_f_b1"], w["hy_f_w2"], w["hy_f_b2"], w["hy_f_w3"],
                          w["hy_f_b3"], w["hy_f_w4"], w["hy_f_freq"], kb)
    yhy = _long_conv(g, x0, kh, w["hy_bias"], w["hy_norm_w"], batch, seqlen, kb)

    w_out = w["w_out"].astype(BF16)
    consts = [row(w["ssd_norm_w"]), w_out[:o1], w_out[o1:], row(w["norm_mix_post"]),
              row(w["norm_ffn_pre"]), w["w_gate"].astype(BF16), w["w_up"].astype(BF16),
              w["w_down"].astype(BF16), row(w["norm_ffn_post"]), row(w["ple_norm_pre"]),
              w["w_ple_gate"].astype(BF16), w["w_ple_proj"].astype(BF16), row(w["ple_norm_post"])]
    out = _mix(yf, yb, z, yhy, x2d, p.reshape(R, -1), consts)
    return out.reshape(batch, seqlen, D)


_WEIGHT_NAMES = (
    "norm_mix_pre", "w_in", "ssd_conv_w", "ssd_conv_b", "ssd_dt_bias", "ssd_a_log", "ssd_d",
    "ssd_norm_w", "hy_conv_w", "hy_conv_b", "hy_f_w1", "hy_f_b1", "hy_f_w2", "hy_f_b2", "hy_f_w3",
    "hy_f_b3", "hy_f_w4", "hy_f_freq", "hy_bias", "hy_norm_w", "w_out", "norm_mix_post",
    "norm_ffn_pre", "w_gate", "w_up", "w_down", "norm_ffn_post", "ple_norm_pre", "w_ple_gate",
    "w_ple_proj", "ple_norm_post")


def _trunk(x, p, weights):
    h = x
    for i in range(p.shape[0]):
        h = _layer(h, p[i], {n: v[i] for n, v in zip(_WEIGHT_NAMES, weights)})
    return h


def kernel(x_prompt, x_sample, p_prompt, p_sample, norm_mix_pre, w_in, ssd_conv_w, ssd_conv_b, ssd_dt_bias, ssd_a_log, ssd_d, ssd_norm_w, hy_conv_w, hy_conv_b, hy_f_w1, hy_f_b1, hy_f_w2, hy_f_b2, hy_f_w3, hy_f_b3, hy_f_w4, hy_f_freq, hy_bias, hy_norm_w, w_out, norm_mix_post, norm_ffn_pre, w_gate, w_up, w_down, norm_ffn_post, ple_norm_pre, w_ple_gate, w_ple_proj, ple_norm_post):
    weights = (norm_mix_pre, w_in, ssd_conv_w, ssd_conv_b, ssd_dt_bias, ssd_a_log, ssd_d, ssd_norm_w,
               hy_conv_w, hy_conv_b, hy_f_w1, hy_f_b1, hy_f_w2, hy_f_b2, hy_f_w3, hy_f_b3, hy_f_w4,
               hy_f_freq, hy_bias, hy_norm_w, w_out, norm_mix_post, norm_ffn_pre, w_gate, w_up,
               w_down, norm_ffn_post, ple_norm_pre, w_ple_gate, w_ple_proj, ple_norm_post)
    return (_trunk(x_prompt, p_prompt, weights), _trunk(x_sample, p_sample, weights))
```

```python
import functools
import math

import numpy as np
import jax
import jax.numpy as jnp
from jax import lax
from jax.experimental import pallas as pl
from jax.experimental.pallas import tpu as pltpu

F32 = jnp.float32
BF16 = jnp.bfloat16
HIGHEST = lax.Precision.HIGHEST

EPS = 1e-6
LANE = 128
SUBLANE = 8
MXU_DIM = 256
VMEM_LIMIT = 56 * 1024 * 1024

SSD_HEAD_DIM = 64
SSD_HEADS = 16
SSD_GROUPS = 2
SSD_STATE = 128
SSD_CHUNK = 128
HY_GROUP = 64
HY_EMB = 33
HY_FAST_DECAY = 0.3
HY_SLOW_DECAY = 1.5
HY_DECAY_TARGET = 1e-2
DFT_N1 = 128
HALO = 16
ROWG = 16


def _params(sem):
    return pltpu.CompilerParams(dimension_semantics=sem, vmem_limit_bytes=VMEM_LIMIT)


def _const_spec(shape):
    nd = len(shape)
    return pl.BlockSpec(shape, lambda *_: (0,) * nd, pipeline_mode=pl.Buffered(1))


def _rms(x, w):
    return x * lax.rsqrt(jnp.mean(x * x, axis=-1, keepdims=True) + EPS) * w


def _silu(x):
    return x * jax.nn.sigmoid(x)


def _softplus(x):
    return jnp.maximum(x, 0.0) + jnp.log1p(jnp.exp(-jnp.abs(x)))


def _split_bf16(x):
    hi = x.astype(BF16)
    return hi, (x - hi.astype(F32)).astype(BF16)


def _dot3(x, w_hi, w_lo):
    x_hi, x_lo = _split_bf16(x)
    dot = lambda a, b: jnp.dot(a, b, preferred_element_type=F32)
    return dot(x_hi, w_hi) + (dot(x_hi, w_lo) + dot(x_lo, w_hi))


def _shifted(xg, o):
    if o == 0:
        return xg
    r = pltpu.roll(xg, (-o) % SUBLANE, axis=1)
    sub = lax.broadcasted_iota(jnp.int32, (1, SUBLANE, 1), 1)
    if o < 0:
        return jnp.where(sub < -o, jnp.concatenate([r[:1], r[:-1]], axis=0), r)
    return jnp.where(sub >= SUBLANE - o, jnp.concatenate([r[1:], r[-1:]], axis=0), r)


def _dwconv(p_ext, cw_ref, cb_ref, cols, width):
    rows, W = p_ext.shape
    xg = p_ext.reshape(rows // SUBLANE, SUBLANE, W)
    acc = cb_ref[:, cols].reshape(1, 1, W)
    for j in range(width):
        acc = acc + cw_ref[j:j + 1, cols].reshape(1, 1, W) * _shifted(xg, j - width // 2)
    hg = HALO // SUBLANE
    return acc[hg:-hg].reshape(rows - 2 * HALO, W)


def _in_proj_kernel(xm_ref, xp_ref, xn_ref, nw_ref, wz_ref, wxbc_ref, why_ref, wdth_ref, wdtl_ref, dtb_ref,
                    scw_ref, scb_ref, hcw_ref, hcb_ref, z_ref, dt_ref, xs_ref, bc_ref, g_ref, x0_ref,
                    *, ntiles, ssd_w, hy_w, cc):
    i = pl.program_id(1)
    tm = xm_ref.shape[0]
    W = xs_ref.shape[1]
    C = g_ref.shape[1]
    xp = jnp.where(i == 0, 0.0, xp_ref[...])
    xn = jnp.where(i == ntiles - 1, 0.0, xn_ref[...])
    u = _rms(jnp.concatenate([xp, xm_ref[...], xn], axis=0), nw_ref[...])
    ub, ul = _split_bf16(u)
    ubm, ulm = ub[HALO:HALO + tm], ul[HALO:HALO + tm]
    dot = lambda a, b: jnp.dot(a, b, preferred_element_type=F32)
    for j in range(0, W, cc):
        z_ref[:, j:j + cc] = dot(ubm, wz_ref[:, j:j + cc]).astype(z_ref.dtype)
    dt_raw = dot(ubm, wdth_ref[...]) + (dot(ubm, wdtl_ref[...]) + dot(ulm, wdth_ref[...]))
    dt_ref[...] = _softplus(dt_raw + dtb_ref[...])
    for c0 in range(0, wxbc_ref.shape[1], cc):
        cols = slice(c0, c0 + cc)
        act = _silu(_dwconv(dot(ub, wxbc_ref[:, cols]), scw_ref, scb_ref, cols, ssd_w))
        if c0 < W:
            xs_ref[:, cols] = act
        else:
            bc_ref[:, c0 - W:c0 - W + cc] = act.astype(BF16)
    for c0 in range(0, C, cc):
        conv = lambda off: _dwconv(dot(ub, why_ref[:, off + c0:off + c0 + cc]), hcw_ref, hcb_ref,
                                   slice(off + c0, off + c0 + cc), hy_w)
        x0_ref[:, c0:c0 + cc] = conv(0)
        g_ref[:, c0:c0 + cc] = conv(2 * C) * conv(C)


def _in_proj(x2d, nw, wz, wxbc, why, wdt, dt_bias, ssd_cw, ssd_cb, hy_cw, hy_cb, batch, seqlen, tm=512, cc=512):
    R, D = x2d.shape
    W = wz.shape[1]
    XW = wxbc.shape[1]
    HW = why.shape[1]
    C = HW // 3
    wdth, wdtl = _split_bf16(wdt)
    pad_w = lambda w: jnp.zeros((SUBLANE, w.shape[1]), F32).at[:w.shape[0]].set(w)
    nt = seqlen // tm
    hb = tm // HALO
    nhalo = R // HALO
    blk = lambda b, i: b * nt + i
    main = lambda w: pl.BlockSpec((tm, w), lambda b, i: (blk(b, i), 0))
    consts = (nw, wz, wxbc, why, wdth, wdtl, dt_bias, pad_w(ssd_cw), ssd_cb.reshape(1, XW),
              pad_w(hy_cw), hy_cb.reshape(1, HW))
    kern = functools.partial(_in_proj_kernel, ntiles=nt, ssd_w=ssd_cw.shape[0], hy_w=hy_cw.shape[0], cc=cc)
    return pl.pallas_call(
        kern,
        grid=(batch, nt),
        in_specs=[main(D),
                  pl.BlockSpec((HALO, D), lambda b, i: (jnp.maximum(blk(b, i) * hb - 1, 0), 0)),
                  pl.BlockSpec((HALO, D), lambda b, i: (jnp.minimum((blk(b, i) + 1) * hb, nhalo - 1), 0))]
        + [_const_spec(a.shape) for a in consts],
        out_specs=[main(W), main(wdt.shape[1]), main(W), main(XW - W), main(C), main(C)],
        out_shape=[jax.ShapeDtypeStruct((R, W), BF16), jax.ShapeDtypeStruct((R, wdt.shape[1]), F32),
                   jax.ShapeDtypeStruct((R, W), F32), jax.ShapeDtypeStruct((R, XW - W), BF16),
                   jax.ShapeDtypeStruct((R, C), F32), jax.ShapeDtypeStruct((R, C), F32)],
        compiler_params=_params(("arbitrary", "arbitrary")),
        name="in_proj",
    )(x2d, x2d, x2d, *consts)


def _ssd_chunks(chunks):
    T = chunks[0][0].shape[0]
    W = SSD_HEADS * SSD_HEAD_DIM
    GN = SSD_GROUPS * SSD_STATE
    GW = W // SSD_GROUPS
    lo_half = lax.broadcasted_iota(jnp.int32, (T, LANE), 1) < SSD_HEAD_DIM
    lo_row = lo_half[0:1]
    col = lambda v, h: jnp.broadcast_to(v[:, h:h + 1], (v.shape[0], LANE))
    heads_per_group = SSD_HEADS // SSD_GROUPS

    pre = []
    for xs_ref, bc_ref, dt_ref, alog_ref, dskip_ref, tri, y_ref, state_ref in chunks:
        dtv = dt_ref[...]
        a = dtv * (-jnp.exp(alog_ref[...]))
        cs = jnp.dot(tri, a, precision=HIGHEST, preferred_element_type=F32)
        tot = jnp.sum(a, axis=0, keepdims=True)
        pre.append((dtv, tri > 0.5, cs, tot, jnp.exp(tot), cs.T))

    for g in range(SSD_GROUPS):
        gl = slice(g * GW, (g + 1) * GW)
        grp = []
        for (xs_ref, bc_ref, _, _, _, _, _, state_ref) in chunks:
            cg = bc_ref[:, GN + g * SSD_STATE:GN + (g + 1) * SSD_STATE]
            bg = bc_ref[:, g * SSD_STATE:(g + 1) * SSD_STATE]
            scores = lax.dot_general(cg, bg, (((1,), (1,)), ((), ())), preferred_element_type=F32)
            st = state_ref[:, gl]
            grp.append((bg, scores, st, jnp.dot(cg, st.astype(BF16), preferred_element_type=F32), [], []))
        for pr in range(heads_per_group // 2):
            h0 = g * heads_per_group + 2 * pr
            ls = slice(h0 * SSD_HEAD_DIM, h0 * SSD_HEAD_DIM + LANE)
            for ch, (dtv, mask, cs, tot, cdec, csT), (bg, scores, st, yoff, xdec, cdec_g) in zip(chunks, pre, grp):
                xs_ref, dskip_ref, y_ref = ch[0], ch[4], ch[6]
                xpair = xs_ref[:, ls]
                bcs = [col(cs, h0), col(cs, h0 + 1)]
                bcs_p = jnp.where(lo_half, bcs[0], bcs[1])
                pidx = jnp.where(lo_half, h0, h0 + 1)
                xdt = xpair * jnp.take_along_axis(dtv, pidx, axis=1)
                tot_p = jnp.where(lo_row, col(tot, h0), col(tot, h0 + 1))
                yp = yoff[:, pr * LANE:(pr + 1) * LANE] * jnp.exp(bcs_p)
                if dskip_ref is not None:
                    yp = yp + xpair * dskip_ref[:, ls]
                for k in range(2):
                    seg = bcs[k] - csT[h0 + k:h0 + k + 1, :]
                    m = (scores * jnp.exp(jnp.where(mask, seg, -1e30))).astype(BF16)
                    xk = jnp.where(lo_half if k == 0 else ~lo_half, xdt, 0.0).astype(BF16)
                    yp = yp + jnp.dot(m, xk, preferred_element_type=F32)
                y_ref[:, ls] = yp.astype(y_ref.dtype)
                xdec.append((xdt * jnp.exp(tot_p - bcs_p)).astype(BF16))
                cdec_g.append(jnp.where(lo_row, col(cdec, h0), col(cdec, h0 + 1)))
        for ch, (bg, scores, st, yoff, xdec, cdec_g) in zip(chunks, grp):
            upd = lax.dot_general(bg, jnp.concatenate(xdec, axis=1), (((0,), (0,)), ((), ())),
                                  preferred_element_type=F32)
            ch[7][:, gl] = st * jnp.concatenate(cdec_g, axis=1) + upd


def _ssd_kernel(xsf_ref, bcf_ref, dtf_ref, xsb_ref, bcb_ref, dtb_ref, alog_ref, dskip_ref, tri_ref,
                yf_ref, yb_ref, state_ref):
    @pl.when(pl.program_id(1) == 0)
    def _():
        state_ref[...] = jnp.zeros_like(state_ref)

    chunks = []
    for k in range(xsf_ref.shape[0]):
        chunks.append((xsf_ref.at[k], bcf_ref.at[k], dtf_ref.at[k], alog_ref.at[0], dskip_ref, tri_ref[0],
                       yf_ref.at[k], state_ref.at[2 * k]))
        chunks.append((xsb_ref.at[k], bcb_ref.at[k], dtb_ref.at[k], alog_ref.at[1], None, tri_ref[1],
                       yb_ref.at[k], state_ref.at[2 * k + 1]))
    _ssd_chunks(chunks)


def _ssd(xs, bc, dt, a_log, ssd_d, batch, seqlen, nrows=2):
    R, W = xs.shape
    T = SSD_CHUNK
    c = seqlen // T
    alog = jnp.zeros((2, 1, LANE), F32).at[:, 0, :SSD_HEADS].set(a_log)
    dskip = jnp.repeat(ssd_d, SSD_HEAD_DIM).reshape(1, W)
    lower = np.tril(np.ones((T, T), np.float32))
    tri = jnp.asarray(np.stack([lower, lower.T]))
    rows4 = lambda a: a.reshape(batch // nrows, nrows, seqlen, a.shape[1])
    fwd = lambda w, lane_blk=0: pl.BlockSpec((None, nrows, T, w), lambda b, i: (b, 0, i, lane_blk))
    bwd = lambda w, lane_blk=0: pl.BlockSpec((None, nrows, T, w), lambda b, i: (b, 0, c - 1 - i, lane_blk))
    xs4, bc4, dt4 = rows4(xs), rows4(bc), rows4(dt)
    yf, yb = pl.pallas_call(
        _ssd_kernel,
        grid=(batch // nrows, c),
        in_specs=[fwd(W), fwd(bc.shape[1]), fwd(LANE, 0), bwd(W), bwd(bc.shape[1]), bwd(LANE, 1),
                  _const_spec(alog.shape), _const_spec(dskip.shape), _const_spec(tri.shape)],
        out_specs=[fwd(W), bwd(W)],
        out_shape=[jax.ShapeDtypeStruct(xs4.shape, BF16)] * 2,
        scratch_shapes=[pltpu.VMEM((2 * nrows, SSD_STATE, W), F32)],
        compiler_params=_params(("arbitrary", "arbitrary")),
        name="ssd",
    )(xs4, bc4, dt4, xs4, bc4, dt4, alog, dskip, tri)
    return yf.reshape(R, W), yb.reshape(R, W)


@functools.lru_cache(maxsize=None)
def _dft_tables(seqlen):
    N = 2 * seqlen
    N1 = DFT_N1
    N2 = N // N1
    K = N1 // 2
    k1 = np.arange(N1).reshape(1, N1, 1)
    n1 = np.arange(K).reshape(1, 1, K)
    n2 = np.arange(N2).reshape(N2, 1, 1)
    ang = -2.0 * np.pi * ((n2 * k1 + N2 * n1 * k1) % N) / N
    gr, gi = np.cos(ang), np.sin(ang)
    gk = np.concatenate([gr, gi], axis=1)
    ga = np.concatenate([np.concatenate([gr, -gi], axis=2),
                         np.concatenate([gi, gr], axis=2)], axis=1)
    grt, git = np.swapaxes(gr, 1, 2), np.swapaxes(gi, 1, 2)
    gc = np.concatenate([np.concatenate([grt, git], axis=2),
                         np.concatenate([-git, grt], axis=2)], axis=1)
    k2 = np.arange(N2).reshape(N2, 1)
    m2 = np.arange(N2).reshape(1, N2)
    ang2 = -2.0 * np.pi * ((k2 * m2) % N2) / N2
    fr, fi = np.cos(ang2), np.sin(ang2)
    fb = np.block([[fr, -fi], [fi, fr]])
    fbc = np.block([[fr, fi], [-fi, fr]])
    to = lambda x: x.astype(np.float32)
    return to(gk), to(ga), to(gc), to(fb), to(fbc)


def _mxu_tables(seqlen):
    return tuple(jnp.asarray(t).astype(BF16) for t in _dft_tables(seqlen))


def _stack_bf16(re, im):
    return jnp.concatenate([re, im], axis=0).astype(BF16)


def _rows_major(x):
    return jnp.swapaxes(x, 0, 1)


def _rows_minor(parts):
    return jnp.swapaxes(jnp.stack(parts, axis=0), 0, 1)


def _filt_a_kernel(fb_ref, w1h_ref, w1l_ref, b1_ref, w2h_ref, w2l_ref, b2_ref, w3h_ref, w3l_ref, b3_ref,
                   fr_ref, w4fh_ref, w4fl_ref, w4bh_ref, w4bl_ref, dl_ref, gk_ref, o_ref, h_ref, *, seqlen):
    L = seqlen
    N1 = o_ref.shape[2]
    K = N1 // 2
    N2 = 2 * L // N1
    rows = ROWG * K
    ridx = lax.broadcasted_iota(jnp.int32, (rows, 1), 0)
    n = N2 * (ridx % K) + (pl.program_id(0) * ROWG + ridx // K)
    pos = n.astype(F32)
    t = pos / (L - 1)

    @pl.when(pl.program_id(1) == 0)
    def _():
        lane = lax.broadcasted_iota(jnp.int32, (1, LANE), 1)
        bands = (HY_EMB - 1) // 2
        ang = fb_ref[...] * (2.0 * math.pi * pos / L)
        z = jnp.where(lane == 0, t,
                      jnp.where(lane <= bands, jnp.cos(ang),
                                jnp.where(lane <= 2 * bands, -jnp.sin(ang), 0.0)))
        fr = fr_ref[...]
        h = jnp.sin(fr * (_dot3(z, w1h_ref[...], w1l_ref[...]) + b1_ref[...]))
        h = jnp.sin(fr * (_dot3(h, w2h_ref[...], w2l_ref[...]) + b2_ref[...]))
        h_ref[...] = jnp.sin(fr * (_dot3(h, w3h_ref[...], w3l_ref[...]) + b3_ref[...]))

    h = h_ref[...]
    win = jnp.exp(-t * dl_ref[...])
    hf = _dot3(h, w4fh_ref[...], w4fl_ref[...]) * win
    hb = _dot3(h, w4bh_ref[...], w4bl_ref[...]) * win
    hb = jnp.where(n == 0, 0.0, hb)
    parts = ((hf + hb).astype(BF16), (hf - hb).astype(BF16))
    for q in range(2):
        ys = [jnp.dot(gk_ref[r], parts[q][r * K:(r + 1) * K], preferred_element_type=F32) for r in range(ROWG)]
        for part in range(2):
            o_ref[q, part] = _rows_minor([y[part * N1:(part + 1) * N1].astype(BF16) for y in ys])


def _filt_b_kernel(a_ref, f_ref, o_ref, *, kb, C, cc, scale):
    N2 = a_ref.shape[3]
    for q in range(kb):
        for c0 in range(0, C, cc):
            cols = slice(c0, c0 + cc)
            for part in range(2):
                x = _stack_bf16(a_ref[part, 0, q, :, cols], a_ref[part, 1, q, :, cols])
                z = jnp.dot(f_ref[part * N2:(part + 1) * N2, :], x, preferred_element_type=F32)
                o_ref[part, q, :, cols] = (z * scale).astype(o_ref.dtype)


def _filter_spectrum(seqlen, w1, b1, w2, b2, w3, b3, w4, freq, kb, cb=256, cc=512):
    L = seqlen
    hidden = w1.shape[1]
    C = w4.shape[1] // 2
    N = 2 * L
    N1 = DFT_N1
    N2 = N // N1
    K = N1 // 2
    ncb = C // cb
    gk, _, _, fb, _ = _mxu_tables(seqlen)
    bands = (HY_EMB - 1) // 2
    fbands = np.zeros((1, LANE), np.float64)
    fbv = np.linspace(1e-4, bands - 1, bands)
    fbands[0, 1:1 + bands] = fbv
    fbands[0, 1 + bands:1 + 2 * bands] = fbv
    max_decay = math.log(HY_DECAY_TARGET) / HY_FAST_DECAY
    min_decay = math.log(HY_DECAY_TARGET) / HY_SLOW_DECAY
    dl = np.abs(np.linspace(min_decay, max_decay, C)).reshape(1, C)
    w1p = jnp.zeros((LANE, hidden), F32).at[:HY_EMB].set(w1)
    row = lambda v: v.reshape(1, hidden)
    w4h, w4l = _split_bf16(w4)
    mlp = (jnp.asarray(fbands, F32), *_split_bf16(w1p), row(b1), *_split_bf16(w2), row(b2),
           *_split_bf16(w3), row(b3), row(freq))
    w4f = pl.BlockSpec((hidden, cb), lambda j, c: (0, c))
    w4b = pl.BlockSpec((hidden, cb), lambda j, c: (0, ncb + c))
    ka = pl.pallas_call(
        functools.partial(_filt_a_kernel, seqlen=L),
        grid=(N2 // ROWG, ncb),
        in_specs=[_const_spec(a.shape) for a in mlp]
        + [w4f, w4f, w4b, w4b, pl.BlockSpec((1, cb), lambda j, c: (0, c)),
           pl.BlockSpec((ROWG, 2 * N1, K), lambda j, c: (j, 0, 0))],
        out_specs=pl.BlockSpec((2, 2, N1, ROWG, cb), lambda j, c: (0, 0, 0, j, c)),
        out_shape=jax.ShapeDtypeStruct((2, 2, N1, N2, C), BF16),
        scratch_shapes=[pltpu.VMEM((ROWG * K, hidden), F32)],
        compiler_params=_params(("arbitrary", "arbitrary")),
        name="filt_a",
    )(*mlp, w4h, w4l, w4h, w4l, jnp.asarray(dl, F32), gk)
    return pl.pallas_call(
        functools.partial(_filt_b_kernel, kb=kb, C=C, cc=cc, scale=1.0 / N),
        grid=(N1 // kb,),
        in_specs=[pl.BlockSpec((2, 2, kb, N2, C), lambda j: (0, 0, j, 0, 0)), _const_spec(fb.shape)],
        out_specs=pl.BlockSpec((2, kb, N2, C), lambda j: (0, j, 0, 0)),
        out_shape=jax.ShapeDtypeStruct((2, N1, N2, C), BF16),
        compiler_params=_params(("arbitrary",)),
        name="filt_b",
    )(ka, fb)


def _fft_a_kernel(x_ref, g_ref, o_ref):
    N1 = o_ref.shape[1]
    xt = [_rows_major(x_ref[part].astype(BF16)) for part in range(2)]
    ys = [jnp.dot(g_ref[r], jnp.concatenate([xt[0][r], xt[1][r]], axis=0), preferred_element_type=F32)
          for r in range(ROWG)]
    for part in range(2):
        o_ref[part] = _rows_minor([y[part * N1:(part + 1) * N1].astype(BF16) for y in ys])


def _fft_b_kernel(a_ref, kh_ref, f_ref, fc_ref, o_ref, *, kb):
    N2 = a_ref.shape[2]
    dot = lambda m, x: jnp.dot(m, x, preferred_element_type=F32)
    zs = [dot(f_ref[...], jnp.concatenate([a_ref[0, q], a_ref[1, q]], axis=0)) for q in range(kb)]
    ws = []
    for q, z in enumerate(zs):
        zr, zi = z[:N2], z[N2:]
        kr, ki = kh_ref[0, q].astype(F32), kh_ref[1, q].astype(F32)
        ws.append(_stack_bf16(zr * kr - zi * ki, zr * ki + zi * kr))
    for q, w in enumerate(ws):
        v = dot(fc_ref[...], w)
        o_ref[0, q] = v[:N2].astype(o_ref.dtype)
        o_ref[1, q] = v[N2:].astype(o_ref.dtype)


def _fft_c_kernel(bt_ref, gc_ref, g_ref, x0_ref, bias_ref, nw_ref, avg_ref, o_ref):
    _, K, _, cb = o_ref.shape
    bt = [_rows_major(bt_ref[part]) for part in range(2)]
    ys = [jnp.dot(gc_ref[r], jnp.concatenate([bt[0][r], bt[1][r]], axis=0), preferred_element_type=F32)
          for r in range(ROWG)]
    rows = 2 * K * ROWG
    conv = jnp.stack([_rows_minor([y[s * K:(s + 1) * K] for y in ys]) for s in range(2)], axis=0)
    flat = lambda x: x.reshape(rows, cb)
    v = flat(x0_ref[...]) * (flat(conv) + flat(g_ref[...]) * bias_ref[...])
    hi, lo = _split_bf16(v * v)
    ms = jnp.concatenate(
        [jnp.dot(hi[:, m:m + MXU_DIM], avg_ref[...], preferred_element_type=F32)
         + jnp.dot(lo[:, m:m + MXU_DIM], avg_ref[...], preferred_element_type=F32)
         for m in range(0, cb, MXU_DIM)], axis=1)
    o_ref[...] = (v * lax.rsqrt(ms + EPS) * nw_ref[...]).reshape(o_ref.shape)


def _long_conv(g, x0, kh, hy_bias, hy_norm_w, batch, seqlen, kb, cb=256):
    R, C = g.shape
    L = seqlen
    N1 = DFT_N1
    N2 = 2 * L // N1
    K = N1 // 2
    bp = batch // 2
    _, ga, gc, fb, fbc = _mxu_tables(L)
    gv = g.reshape(bp, 2, K, N2, C)
    x0v = x0.reshape(bp, 2, K, N2, C)
    half = pl.BlockSpec((None, 2, K, ROWG, cb), lambda p, j, c: (p, 0, 0, j, c))
    full = pl.BlockSpec((None, 2, N1, ROWG, cb), lambda p, j, c: (p, 0, 0, j, c))
    grid_ac = (bp, N2 // ROWG, C // cb)
    a = pl.pallas_call(
        _fft_a_kernel,
        grid=grid_ac,
        in_specs=[half, pl.BlockSpec((ROWG, 2 * N1, N1), lambda p, j, c: (j, 0, 0))],
        out_specs=full,
        out_shape=jax.ShapeDtypeStruct((bp, 2, N1, N2, C), BF16),
        compiler_params=_params(("arbitrary",) * 3),
        name="fft_a",
    )(gv, ga)
    bt = pl.pallas_call(
        functools.partial(_fft_b_kernel, kb=kb),
        grid=(N1 // kb, bp),
        in_specs=[pl.BlockSpec((None, 2, kb, N2, C), lambda j, p: (p, 0, j, 0, 0)),
                  pl.BlockSpec((2, kb, N2, C), lambda j, p: (0, j, 0, 0)),
                  _const_spec(fb.shape), _const_spec(fbc.shape)],
        out_specs=pl.BlockSpec((None, 2, kb, N2, C), lambda j, p: (p, 0, j, 0, 0)),
        out_shape=jax.ShapeDtypeStruct((bp, 2, N1, N2, C), BF16),
        compiler_params=_params(("arbitrary", "arbitrary")),
        name="fft_b",
    )(a, kh, fb, fbc)
    avg = np.kron(np.eye(MXU_DIM // HY_GROUP), np.full((HY_GROUP, HY_GROUP), 1.0 / HY_GROUP))
    vec = pl.BlockSpec((1, cb), lambda p, j, c: (0, c))
    y = pl.pallas_call(
        _fft_c_kernel,
        grid=grid_ac,
        in_specs=[full, pl.BlockSpec((ROWG, N1, 2 * N1), lambda p, j, c: (j, 0, 0)),
                  half, half, vec, vec, _const_spec((MXU_DIM, MXU_DIM))],
        out_specs=half,
        out_shape=jax.ShapeDtypeStruct((bp, 2, K, N2, C), F32),
        compiler_params=_params(("arbitrary",) * 3),
        name="fft_c",
    )(bt, gc, gv, x0v, hy_bias.reshape(1, C), hy_norm_w.reshape(1, C),
      jnp.asarray(avg.astype(np.float32)).astype(BF16))
    return y.reshape(R, C)


def _mix_kernel(yf_ref, yb_ref, z_ref, yhy_ref, h_ref, p_ref, snw_ref, wos_ref, woh_ref, nmp_ref,
                nfp_ref, wg_ref, wu_ref, wd_ref, nfo_ref, npp_ref, wpg_ref, wpp_ref, npo_ref, o_ref, *, nsub):
    dot = lambda x, w_ref: jnp.dot(x.astype(BF16), w_ref[...], preferred_element_type=F32)
    sub = o_ref.shape[0] // nsub
    tiles = [slice(k * sub, (k + 1) * sub) for k in range(nsub)]
    gw = yf_ref.shape[1] // SSD_GROUPS

    def gated_norm(r):
        ys = (yf_ref[r, :].astype(F32) + yb_ref[r, :].astype(F32)) * _silu(z_ref[r, :].astype(F32))
        return jnp.concatenate(
            [_rms(ys[:, k * gw:(k + 1) * gw], snw_ref[:, k * gw:(k + 1) * gw]) for k in range(SSD_GROUPS)],
            axis=1)

    ysn = [gated_norm(r) for r in tiles]
    mix = [dot(y, wos_ref) + dot(yhy_ref[r, :], woh_ref) for y, r in zip(ysn, tiles)]
    h1 = [h_ref[r, :] + _rms(m, nmp_ref[...]) for m, r in zip(mix, tiles)]
    u = [_rms(x, nfp_ref[...]).astype(BF16) for x in h1]
    act = [(_silu(dot(x, wg_ref)) * dot(x, wu_ref)) for x in u]
    ff = [dot(a, wd_ref) for a in act]
    h2 = [x + _rms(f, nfo_ref[...]) for x, f in zip(h1, ff)]
    gate = [jax.nn.sigmoid(dot(_rms(x, npp_ref[...]), wpg_ref)) for x in h2]
    emb = [dot(p_ref[r, :], wpp_ref) for r in tiles]
    for r, x, gt, e in zip(tiles, h2, gate, emb):
        o_ref[r, :] = x + _rms(gt * e, npo_ref[...])


def _mix(yf, yb, z, yhy, h, p, consts, tm=512, nsub=2):
    R, D = h.shape
    row = lambda w: pl.BlockSpec((tm, w), lambda i: (i, 0))
    return pl.pallas_call(
        functools.partial(_mix_kernel, nsub=nsub),
        grid=(R // tm,),
        in_specs=[row(D), row(D), row(D), row(D), row(D), row(p.shape[1])]
        + [_const_spec(c.shape) for c in consts],
        out_specs=row(D),
        out_shape=jax.ShapeDtypeStruct((R, D), F32),
        compiler_params=_params(("arbitrary",)),
        name="mix",
    )(yf, yb, z, yhy, h, p, *consts)


def _layer(x, p, w):
    batch, seqlen, D = x.shape
    R = batch * seqlen
    x2d = x.reshape(R, D)
    row = lambda v: v.reshape(1, -1)
    o1 = SSD_HEADS * SSD_HEAD_DIM
    o2 = o1 + w["ssd_conv_w"].shape[1]
    o3 = o2 + 2 * SSD_HEADS
    w_in = w["w_in"]
    H = SSD_HEADS
    wdt = jnp.zeros((D, 2 * LANE), F32)
    wdt = wdt.at[:, :H].set(w_in[:, o2:o2 + H]).at[:, LANE:LANE + H].set(w_in[:, o2 + H:o3])
    dtb = jnp.zeros((1, 2 * LANE), F32)
    dtb = dtb.at[0, :H].set(w["ssd_dt_bias"][0]).at[0, LANE:LANE + H].set(w["ssd_dt_bias"][1])
    z, dt, xs, bc, g, x0 = _in_proj(
        x2d, row(w["norm_mix_pre"]), w_in[:, :o1].astype(BF16), w_in[:, o1:o2].astype(BF16),
        w_in[:, o3:].astype(BF16), wdt, dtb, w["ssd_conv_w"], w["ssd_conv_b"], w["hy_conv_w"],
        w["hy_conv_b"], batch, seqlen)
    yf, yb = _ssd(xs, bc, dt, w["ssd_a_log"], w["ssd_d"], batch, seqlen)

    n2 = 2 * seqlen // DFT_N1
    kb = max(1, min(DFT_N1, 256 // n2))
    kh = _filter_spectrum(seqlen, w["hy_f_w1"], w["hy_f_b1"], w["hy_f_w2"], w["hy_f_b2"], w["hy_f_w3"],
                          w["hy_f_b3"], w["hy_f_w4"], w["hy_f_freq"], kb)
    yhy = _long_conv(g, x0, kh, w["hy_bias"], w["hy_norm_w"], batch, seqlen, kb)

    w_out = w["w_out"].astype(BF16)
    consts = [row(w["ssd_norm_w"]), w_out[:o1], w_out[o1:], row(w["norm_mix_post"]),
              row(w["norm_ffn_pre"]), w["w_gate"].astype(BF16), w["w_up"].astype(BF16),
              w["w_down"].astype(BF16), row(w["norm_ffn_post"]), row(w["ple_norm_pre"]),
              w["w_ple_gate"].astype(BF16), w["w_ple_proj"].astype(BF16), row(w["ple_norm_post"])]
    out = _mix(yf, yb, z, yhy, x2d, p.reshape(R, -1), consts)
    return out.reshape(batch, seqlen, D)


_WEIGHT_NAMES = (
    "norm_mix_pre", "w_in", "ssd_conv_w", "ssd_conv_b", "ssd_dt_bias", "ssd_a_log", "ssd_d",
    "ssd_norm_w", "hy_conv_w", "hy_conv_b", "hy_f_w1", "hy_f_b1", "hy_f_w2", "hy_f_b2", "hy_f_w3",
    "hy_f_b3", "hy_f_w4", "hy_f_freq", "hy_bias", "hy_norm_w", "w_out", "norm_mix_post",
    "norm_ffn_pre", "w_gate", "w_up", "w_down", "norm_ffn_post", "ple_norm_pre", "w_ple_gate",
    "w_ple_proj", "ple_norm_post")


def _trunk(x, p, weights):
    h = x
    for i in range(p.shape[0]):
        h = _layer(h, p[i], {n: v[i] for n, v in zip(_WEIGHT_NAMES, weights)})
    return h


def kernel(x_prompt, x_sample, p_prompt, p_sample, norm_mix_pre, w_in, ssd_conv_w, ssd_conv_b, ssd_dt_bias, ssd_a_log, ssd_d, ssd_norm_w, hy_conv_w, hy_conv_b, hy_f_w1, hy_f_b1, hy_f_w2, hy_f_b2, hy_f_w3, hy_f_b3, hy_f_w4, hy_f_freq, hy_bias, hy_norm_w, w_out, norm_mix_post, norm_ffn_pre, w_gate, w_up, w_down, norm_ffn_post, ple_norm_pre, w_ple_gate, w_ple_proj, ple_norm_post):
    weights = (norm_mix_pre, w_in, ssd_conv_w, ssd_conv_b, ssd_dt_bias, ssd_a_log, ssd_d, ssd_norm_w,
               hy_conv_w, hy_conv_b, hy_f_w1, hy_f_b1, hy_f_w2, hy_f_b2, hy_f_w3, hy_f_b3, hy_f_w4,
               hy_f_freq, hy_bias, hy_norm_w, w_out, norm_mix_post, norm_ffn_pre, w_gate, w_up,
               w_down, norm_ffn_post, ple_norm_pre, w_ple_gate, w_ple_proj, ple_norm_post)
    return (_trunk(x_prompt, p_prompt, weights), _trunk(x_sample, p_sample, weights))
```

```python
import functools
import math

import numpy as np
import jax
import jax.numpy as jnp
from jax import lax
from jax.experimental import pallas as pl
from jax.experimental.pallas import tpu as pltpu

F32 = jnp.float32
BF16 = jnp.bfloat16
HIGHEST = lax.Precision.HIGHEST

EPS = 1e-6
LANE = 128
SUBLANE = 8
MXU_DIM = 256
VMEM_LIMIT = 56 * 1024 * 1024

SSD_HEAD_DIM = 64
SSD_HEADS = 16
SSD_GROUPS = 2
SSD_STATE = 128
SSD_CHUNK = 128
HY_GROUP = 64
HY_EMB = 33
HY_FAST_DECAY = 0.3
HY_SLOW_DECAY = 1.5
HY_DECAY_TARGET = 1e-2
DFT_N1 = 128
HALO = 16
ROWG = 16


def _params(sem):
    return pltpu.CompilerParams(dimension_semantics=sem, vmem_limit_bytes=VMEM_LIMIT)


def _const_spec(shape):
    nd = len(shape)
    return pl.BlockSpec(shape, lambda *_: (0,) * nd, pipeline_mode=pl.Buffered(1))


def _rms(x, w):
    return x * lax.rsqrt(jnp.mean(x * x, axis=-1, keepdims=True) + EPS) * w


def _silu(x):
    return x * jax.nn.sigmoid(x)


def _softplus(x):
    return jnp.maximum(x, 0.0) + jnp.log1p(jnp.exp(-jnp.abs(x)))


def _split_bf16(x):
    hi = x.astype(BF16)
    return hi, (x - hi.astype(F32)).astype(BF16)


def _dot3(x, w_hi, w_lo):
    x_hi, x_lo = _split_bf16(x)
    dot = lambda a, b: jnp.dot(a, b, preferred_element_type=F32)
    return dot(x_hi, w_hi) + (dot(x_hi, w_lo) + dot(x_lo, w_hi))


def _shifted(xg, o):
    if o == 0:
        return xg
    sub = lax.broadcasted_iota(jnp.int32, (1, SUBLANE, 1), 1)
    if o < 0:
        w = jnp.where(sub >= SUBLANE + o, jnp.concatenate([xg[:1], xg[:-1]], axis=0), xg)
    else:
        w = jnp.where(sub < o, jnp.concatenate([xg[1:], xg[-1:]], axis=0), xg)
    return pltpu.roll(w, (-o) % SUBLANE, axis=1)


def _dwconv(p_ext, cw_ref, cb_ref, cols, width):
    rows, W = p_ext.shape
    xg = p_ext.reshape(rows // SUBLANE, SUBLANE, W)
    acc = cb_ref[:, cols].reshape(1, 1, W)
    for j in range(width):
        acc = acc + cw_ref[j:j + 1, cols].reshape(1, 1, W) * _shifted(xg, j - width // 2)
    hg = HALO // SUBLANE
    return acc[hg:-hg].reshape(rows - 2 * HALO, W)


def _in_proj_kernel(xm_ref, xp_ref, xn_ref, nw_ref, wz_ref, wxbc_ref, why_ref, wdth_ref, wdtl_ref, dtb_ref,
                    scw_ref, scb_ref, hcw_ref, hcb_ref, z_ref, dt_ref, xs_ref, bc_ref, g_ref, x0_ref,
                    *, ntiles, ssd_w, hy_w, cc):
    i = pl.program_id(1)
    tm = xm_ref.shape[0]
    W = xs_ref.shape[1]
    C = g_ref.shape[1]
    xp = jnp.where(i == 0, 0.0, xp_ref[...])
    xn = jnp.where(i == ntiles - 1, 0.0, xn_ref[...])
    u = _rms(jnp.concatenate([xp, xm_ref[...], xn], axis=0), nw_ref[...])
    ub, ul = _split_bf16(u)
    ubm, ulm = ub[HALO:HALO + tm], ul[HALO:HALO + tm]
    dot = lambda a, b: jnp.dot(a, b, preferred_element_type=F32)
    for j in range(0, W, cc):
        z_ref[:, j:j + cc] = dot(ubm, wz_ref[:, j:j + cc]).astype(z_ref.dtype)
    dt_raw = dot(ubm, wdth_ref[...]) + (dot(ubm, wdtl_ref[...]) + dot(ulm, wdth_ref[...]))
    dt_ref[...] = _softplus(dt_raw + dtb_ref[...])
    for c0 in range(0, wxbc_ref.shape[1], cc):
        cols = slice(c0, c0 + cc)
        act = _silu(_dwconv(dot(ub, wxbc_ref[:, cols]), scw_ref, scb_ref, cols, ssd_w))
        if c0 < W:
            xs_ref[:, cols] = act
        else:
            bc_ref[:, c0 - W:c0 - W + cc] = act.astype(BF16)
    for c0 in range(0, C, cc):
        conv = lambda off: _dwconv(dot(ub, why_ref[:, off + c0:off + c0 + cc]), hcw_ref, hcb_ref,
                                   slice(off + c0, off + c0 + cc), hy_w)
        x0_ref[:, c0:c0 + cc] = conv(0)
        g_ref[:, c0:c0 + cc] = conv(2 * C) * conv(C)


def _in_proj(x2d, nw, wz, wxbc, why, wdt, dt_bias, ssd_cw, ssd_cb, hy_cw, hy_cb, batch, seqlen, tm=512, cc=512):
    R, D = x2d.shape
    W = wz.shape[1]
    XW = wxbc.shape[1]
    HW = why.shape[1]
    C = HW // 3
    wdth, wdtl = _split_bf16(wdt)
    pad_w = lambda w: jnp.zeros((SUBLANE, w.shape[1]), F32).at[:w.shape[0]].set(w)
    nt = seqlen // tm
    hb = tm // HALO
    nhalo = R // HALO
    blk = lambda b, i: b * nt + i
    main = lambda w: pl.BlockSpec((tm, w), lambda b, i: (blk(b, i), 0))
    consts = (nw, wz, wxbc, why, wdth, wdtl, dt_bias, pad_w(ssd_cw), ssd_cb.reshape(1, XW),
              pad_w(hy_cw), hy_cb.reshape(1, HW))
    kern = functools.partial(_in_proj_kernel, ntiles=nt, ssd_w=ssd_cw.shape[0], hy_w=hy_cw.shape[0], cc=cc)
    return pl.pallas_call(
        kern,
        grid=(batch, nt),
        in_specs=[main(D),
                  pl.BlockSpec((HALO, D), lambda b, i: (jnp.maximum(blk(b, i) * hb - 1, 0), 0)),
                  pl.BlockSpec((HALO, D), lambda b, i: (jnp.minimum((blk(b, i) + 1) * hb, nhalo - 1), 0))]
        + [_const_spec(a.shape) for a in consts],
        out_specs=[main(W), main(wdt.shape[1]), main(W), main(XW - W), main(C), main(C)],
        out_shape=[jax.ShapeDtypeStruct((R, W), BF16), jax.ShapeDtypeStruct((R, wdt.shape[1]), F32),
                   jax.ShapeDtypeStruct((R, W), F32), jax.ShapeDtypeStruct((R, XW - W), BF16),
                   jax.ShapeDtypeStruct((R, C), F32), jax.ShapeDtypeStruct((R, C), F32)],
        compiler_params=_params(("arbitrary", "arbitrary")),
        name="in_proj",
    )(x2d, x2d, x2d, *consts)


def _ssd_chunks(chunks):
    T = chunks[0][0].shape[0]
    W = SSD_HEADS * SSD_HEAD_DIM
    GN = SSD_GROUPS * SSD_STATE
    GW = W // SSD_GROUPS
    lo_half = lax.broadcasted_iota(jnp.int32, (T, LANE), 1) < SSD_HEAD_DIM
    lo_row = lo_half[0:1]
    col = lambda v, h: jnp.broadcast_to(v[:, h:h + 1], (v.shape[0], LANE))
    heads_per_group = SSD_HEADS // SSD_GROUPS

    pre = []
    for xs_ref, bc_ref, dt_ref, alog_ref, dskip_ref, tri, y_ref, state_ref in chunks:
        dtv = dt_ref[...]
        a = dtv * (-jnp.exp(alog_ref[...]))
        cs = jnp.dot(tri, a, precision=HIGHEST, preferred_element_type=F32)
        tot = jnp.sum(a, axis=0, keepdims=True)
        pre.append((dtv, tri > 0.5, cs, tot, jnp.exp(tot), cs.T))

    for g in range(SSD_GROUPS):
        gl = slice(g * GW, (g + 1) * GW)
        grp = []
        for (xs_ref, bc_ref, _, _, _, _, _, state_ref) in chunks:
            cg = bc_ref[:, GN + g * SSD_STATE:GN + (g + 1) * SSD_STATE]
            bg = bc_ref[:, g * SSD_STATE:(g + 1) * SSD_STATE]
            scores = lax.dot_general(cg, bg, (((1,), (1,)), ((), ())), preferred_element_type=F32)
            st = state_ref[:, gl]
            grp.append((bg, scores, st, jnp.dot(cg, st.astype(BF16), preferred_element_type=F32), [], []))
        for pr in range(heads_per_group // 2):
            h0 = g * heads_per_group + 2 * pr
            ls = slice(h0 * SSD_HEAD_DIM, h0 * SSD_HEAD_DIM + LANE)
            for ch, (dtv, mask, cs, tot, cdec, csT), (bg, scores, st, yoff, xdec, cdec_g) in zip(chunks, pre, grp):
                xs_ref, dskip_ref, y_ref = ch[0], ch[4], ch[6]
                xpair = xs_ref[:, ls]
                bcs = [col(cs, h0), col(cs, h0 + 1)]
                bcs_p = jnp.where(lo_half, bcs[0], bcs[1])
                pidx = jnp.where(lo_half, h0, h0 + 1)
                xdt = xpair * jnp.take_along_axis(dtv, pidx, axis=1)
                tot_p = jnp.where(lo_row, col(tot, h0), col(tot, h0 + 1))
                yp = yoff[:, pr * LANE:(pr + 1) * LANE] * jnp.exp(bcs_p)
                if dskip_ref is not None:
                    yp = yp + xpair * dskip_ref[:, ls]
                for k in range(2):
                    seg = bcs[k] - csT[h0 + k:h0 + k + 1, :]
                    m = (scores * jnp.exp(jnp.where(mask, seg, -1e30))).astype(BF16)
                    xk = jnp.where(lo_half if k == 0 else ~lo_half, xdt, 0.0).astype(BF16)
                    yp = yp + jnp.dot(m, xk, preferred_element_type=F32)
                y_ref[:, ls] = yp.astype(y_ref.dtype)
                xdec.append((xdt * jnp.exp(tot_p - bcs_p)).astype(BF16))
                cdec_g.append(jnp.where(lo_row, col(cdec, h0), col(cdec, h0 + 1)))
        for ch, (bg, scores, st, yoff, xdec, cdec_g) in zip(chunks, grp):
            upd = lax.dot_general(bg, jnp.concatenate(xdec, axis=1), (((0,), (0,)), ((), ())),
                                  preferred_element_type=F32)
            ch[7][:, gl] = st * jnp.concatenate(cdec_g, axis=1) + upd


def _ssd_kernel(xsf_ref, bcf_ref, dtf_ref, xsb_ref, bcb_ref, dtb_ref, alog_ref, dskip_ref, tri_ref,
                yf_ref, yb_ref, state_ref):
    @pl.when(pl.program_id(1) == 0)
    def _():
        state_ref[...] = jnp.zeros_like(state_ref)

    chunks = []
    for k in range(xsf_ref.shape[0]):
        chunks.append((xsf_ref.at[k], bcf_ref.at[k], dtf_ref.at[k], alog_ref.at[0], dskip_ref, tri_ref[0],
                       yf_ref.at[k], state_ref.at[2 * k]))
        chunks.append((xsb_ref.at[k], bcb_ref.at[k], dtb_ref.at[k], alog_ref.at[1], None, tri_ref[1],
                       yb_ref.at[k], state_ref.at[2 * k + 1]))
    _ssd_chunks(chunks)


def _ssd(xs, bc, dt, a_log, ssd_d, batch, seqlen, nrows=2):
    R, W = xs.shape
    T = SSD_CHUNK
    c = seqlen // T
    alog = jnp.zeros((2, 1, LANE), F32).at[:, 0, :SSD_HEADS].set(a_log)
    dskip = jnp.repeat(ssd_d, SSD_HEAD_DIM).reshape(1, W)
    lower = np.tril(np.ones((T, T), np.float32))
    tri = jnp.asarray(np.stack([lower, lower.T]))
    rows4 = lambda a: a.reshape(batch // nrows, nrows, seqlen, a.shape[1])
    fwd = lambda w, lane_blk=0: pl.BlockSpec((None, nrows, T, w), lambda b, i: (b, 0, i, lane_blk))
    bwd = lambda w, lane_blk=0: pl.BlockSpec((None, nrows, T, w), lambda b, i: (b, 0, c - 1 - i, lane_blk))
    xs4, bc4, dt4 = rows4(xs), rows4(bc), rows4(dt)
    yf, yb = pl.pallas_call(
        _ssd_kernel,
        grid=(batch // nrows, c),
        in_specs=[fwd(W), fwd(bc.shape[1]), fwd(LANE, 0), bwd(W), bwd(bc.shape[1]), bwd(LANE, 1),
                  _const_spec(alog.shape), _const_spec(dskip.shape), _const_spec(tri.shape)],
        out_specs=[fwd(W), bwd(W)],
        out_shape=[jax.ShapeDtypeStruct(xs4.shape, BF16)] * 2,
        scratch_shapes=[pltpu.VMEM((2 * nrows, SSD_STATE, W), F32)],
        compiler_params=_params(("arbitrary", "arbitrary")),
        name="ssd",
    )(xs4, bc4, dt4, xs4, bc4, dt4, alog, dskip, tri)
    return yf.reshape(R, W), yb.reshape(R, W)


@functools.lru_cache(maxsize=None)
def _dft_tables(seqlen):
    N = 2 * seqlen
    N1 = DFT_N1
    N2 = N // N1
    K = N1 // 2
    k1 = np.arange(N1).reshape(1, N1, 1)
    n1 = np.arange(K).reshape(1, 1, K)
    n2 = np.arange(N2).reshape(N2, 1, 1)
    ang = -2.0 * np.pi * ((n2 * k1 + N2 * n1 * k1) % N) / N
    gr, gi = np.cos(ang), np.sin(ang)
    gk = np.concatenate([gr, gi], axis=1)
    ga = np.concatenate([np.concatenate([gr, -gi], axis=2),
                         np.concatenate([gi, gr], axis=2)], axis=1)
    grt, git = np.swapaxes(gr, 1, 2), np.swapaxes(gi, 1, 2)
    gc = np.concatenate([np.concatenate([grt, git], axis=2),
                         np.concatenate([-git, grt], axis=2)], axis=1)
    k2 = np.arange(N2).reshape(N2, 1)
    m2 = np.arange(N2).reshape(1, N2)
    ang2 = -2.0 * np.pi * ((k2 * m2) % N2) / N2
    fr, fi = np.cos(ang2), np.sin(ang2)
    fb = np.block([[fr, -fi], [fi, fr]])
    fbc = np.block([[fr, fi], [-fi, fr]])
    to = lambda x: x.astype(np.float32)
    return to(gk), to(ga), to(gc), to(fb), to(fbc)


def _mxu_tables(seqlen):
    return tuple(jnp.asarray(t).astype(BF16) for t in _dft_tables(seqlen))


def _stack_bf16(re, im):
    return jnp.concatenate([re, im], axis=0).astype(BF16)


def _rows_major(x):
    return jnp.swapaxes(x, 0, 1)


def _rows_minor(parts):
    return jnp.swapaxes(jnp.stack(parts, axis=0), 0, 1)


def _filt_a_kernel(fb_ref, w1ah_ref, w1al_ref, w1bh_ref, w1bl_ref, b1_ref, w2h_ref, w2l_ref, b2_ref,
                   w3h_ref, w3l_ref, b3_ref, fr_ref, w4fh_ref, w4fl_ref, w4bh_ref, w4bl_ref, dl_ref, gk_ref,
                   o_ref, h_ref, *, seqlen):
    L = seqlen
    N1 = o_ref.shape[3]
    K = N1 // 2
    N2 = 2 * L // N1
    rows = ROWG * K
    half = rows // 2
    ridx = lax.broadcasted_iota(jnp.int32, (rows, 1), 0)
    n = N2 * (ridx % K) + (pl.program_id(0) * ROWG + ridx // K)
    pos = n.astype(F32)
    t = pos / (L - 1)

    @pl.when(pl.program_id(1) == 0)
    def _():
        lane = lax.broadcasted_iota(jnp.int32, (1, LANE), 1)
        bands = (HY_EMB - 1) // 2
        ang = fb_ref[...] * (2.0 * math.pi * pos / L)
        z = jnp.where(lane == 0, t,
                      jnp.where(lane <= bands, jnp.cos(ang),
                                jnp.where(lane <= 2 * bands, -jnp.sin(ang), 0.0)))
        fr = fr_ref[...]
        pre = _dot3(z[:half], w1ah_ref[...], w1al_ref[...]) + _dot3(z[half:], w1bh_ref[...], w1bl_ref[...])
        h = jnp.sin(fr * (pre + b1_ref[...]))
        h = jnp.sin(fr * (_dot3(h, w2h_ref[...], w2l_ref[...]) + b2_ref[...]))
        h_ref[...] = jnp.sin(fr * (_dot3(h, w3h_ref[...], w3l_ref[...]) + b3_ref[...]))

    h = h_ref[...]
    win = jnp.exp(-t * dl_ref[...])

    def filt(wh_ref, wl_ref):
        top = _dot3(h, wh_ref[0], wl_ref[0])
        bot = _dot3(h, wh_ref[1], wl_ref[1])
        return jnp.concatenate([top, bot], axis=0) * win

    hf = filt(w4fh_ref, w4fl_ref)
    hb = jnp.where(n == 0, 0.0, filt(w4bh_ref, w4bl_ref))
    parts = ((hf + hb).astype(BF16), (hf - hb).astype(BF16))
    for q in range(2):
        for r in range(ROWG):
            y = jnp.dot(gk_ref[r], parts[q][r * K:(r + 1) * K], preferred_element_type=F32)
            o_ref[q, 0, r] = y[:N1].astype(BF16)
            o_ref[q, 1, r] = y[N1:].astype(BF16)


def _filt_b_kernel(a_ref, f_ref, o_ref, *, scale):
    N2 = a_ref.shape[2]
    for part in range(2):
        re, im = _rows_major(a_ref[part, 0]), _rows_major(a_ref[part, 1])
        for q in range(ROWG):
            z = jnp.dot(f_ref[part * N2:(part + 1) * N2, :], jnp.concatenate([re[q], im[q]], axis=0),
                        preferred_element_type=F32)
            o_ref[part, q] = (z * scale).astype(o_ref.dtype)


def _filter_spectrum(seqlen, w1, b1, w2, b2, w3, b3, w4, freq, cb=256):
    L = seqlen
    hidden = w1.shape[1]
    C = w4.shape[1] // 2
    N = 2 * L
    N1 = DFT_N1
    N2 = N // N1
    K = N1 // 2
    ncb = C // cb
    gk, _, _, fb, _ = _mxu_tables(seqlen)
    bands = (HY_EMB - 1) // 2
    fbands = np.zeros((1, LANE), np.float64)
    fbv = np.linspace(1e-4, bands - 1, bands)
    fbands[0, 1:1 + bands] = fbv
    fbands[0, 1 + bands:1 + 2 * bands] = fbv
    max_decay = math.log(HY_DECAY_TARGET) / HY_FAST_DECAY
    min_decay = math.log(HY_DECAY_TARGET) / HY_SLOW_DECAY
    dl = np.abs(np.linspace(min_decay, max_decay, C)).reshape(1, C)
    zeros = lambda r, c: jnp.zeros((r, c), F32)
    w1p = zeros(LANE, hidden).at[:HY_EMB].set(w1)
    w1a = jnp.concatenate([w1p, zeros(LANE, hidden)], axis=1)
    w1b = jnp.concatenate([zeros(LANE, hidden), w1p], axis=1)
    bdiag = lambda w: jnp.concatenate([jnp.concatenate([w, zeros(hidden, hidden)], axis=1),
                                       jnp.concatenate([zeros(hidden, hidden), w], axis=1)], axis=0)
    row2 = lambda v: jnp.concatenate([v, v]).reshape(1, 2 * hidden)
    w4pad = jnp.stack([jnp.concatenate([w4, zeros(hidden, 2 * C)], axis=0),
                       jnp.concatenate([zeros(hidden, 2 * C), w4], axis=0)])
    w4h, w4l = _split_bf16(w4pad)
    mlp = (jnp.asarray(fbands, F32), *_split_bf16(w1a), *_split_bf16(w1b), row2(b1), *_split_bf16(bdiag(w2)),
           row2(b2), *_split_bf16(bdiag(w3)), row2(b3), row2(freq))
    w4f = pl.BlockSpec((2, 2 * hidden, cb), lambda j, c: (0, 0, c))
    w4b = pl.BlockSpec((2, 2 * hidden, cb), lambda j, c: (0, 0, ncb + c))
    ka = pl.pallas_call(
        functools.partial(_filt_a_kernel, seqlen=L),
        grid=(N2 // ROWG, ncb),
        in_specs=[_const_spec(a.shape) for a in mlp]
        + [w4f, w4f, w4b, w4b, pl.BlockSpec((1, cb), lambda j, c: (0, c)),
           pl.BlockSpec((ROWG, 2 * N1, K), lambda j, c: (j, 0, 0))],
        out_specs=pl.BlockSpec((2, 2, ROWG, N1, cb), lambda j, c: (0, 0, j, 0, c)),
        out_shape=jax.ShapeDtypeStruct((2, 2, N2, N1, C), BF16),
        scratch_shapes=[pltpu.VMEM((ROWG * K // 2, 2 * hidden), F32)],
        compiler_params=_params(("arbitrary", "arbitrary")),
        name="filt_a",
    )(*mlp, w4h, w4l, w4h, w4l, jnp.asarray(dl, F32), gk)
    return pl.pallas_call(
        functools.partial(_filt_b_kernel, scale=1.0 / N),
        grid=(N1 // ROWG, ncb),
        in_specs=[pl.BlockSpec((2, 2, N2, ROWG, cb), lambda j, c: (0, 0, 0, j, c)), _const_spec(fb.shape)],
        out_specs=pl.BlockSpec((2, ROWG, N2, cb), lambda j, c: (0, j, 0, c)),
        out_shape=jax.ShapeDtypeStruct((2, N1, N2, C), BF16),
        compiler_params=_params(("arbitrary", "arbitrary")),
        name="filt_b",
    )(ka, fb)


def _fft_a_kernel(x_ref, g_ref, o_ref):
    N1 = o_ref.shape[1]
    xt = [_rows_major(x_ref[part].astype(BF16)) for part in range(2)]
    ys = [jnp.dot(g_ref[r], jnp.concatenate([xt[0][r], xt[1][r]], axis=0), preferred_element_type=F32)
          for r in range(ROWG)]
    for part in range(2):
        o_ref[part] = _rows_minor([y[part * N1:(part + 1) * N1].astype(BF16) for y in ys])


def _fft_b_kernel(a_ref, kh_ref, f_ref, fc_ref, o_ref, *, kb):
    N2 = a_ref.shape[2]
    dot = lambda m, x: jnp.dot(m, x, preferred_element_type=F32)
    zs = [dot(f_ref[...], jnp.concatenate([a_ref[0, q], a_ref[1, q]], axis=0)) for q in range(kb)]
    ws = []
    for q, z in enumerate(zs):
        zr, zi = z[:N2], z[N2:]
        kr, ki = kh_ref[0, q].astype(F32), kh_ref[1, q].astype(F32)
        ws.append(_stack_bf16(zr * kr - zi * ki, zr * ki + zi * kr))
    for q, w in enumerate(ws):
        v = dot(fc_ref[...], w)
        o_ref[0, q] = v[:N2].astype(o_ref.dtype)
        o_ref[1, q] = v[N2:].astype(o_ref.dtype)


def _fft_c_kernel(bt_ref, gc_ref, g_ref, x0_ref, bias_ref, nw_ref, avg_ref, o_ref):
    _, K, _, cb = o_ref.shape
    bt = [_rows_major(bt_ref[part]) for part in range(2)]
    ys = [jnp.dot(gc_ref[r], jnp.concatenate([bt[0][r], bt[1][r]], axis=0), preferred_element_type=F32)
          for r in range(ROWG)]
    rows = 2 * K * ROWG
    conv = jnp.stack([_rows_minor([y[s * K:(s + 1) * K] for y in ys]) for s in range(2)], axis=0)
    flat = lambda x: x.reshape(rows, cb)
    v = flat(x0_ref[...]) * (flat(conv) + flat(g_ref[...]) * bias_ref[...])
    hi, lo = _split_bf16(v * v)
    ms = jnp.concatenate(
        [jnp.dot(hi[:, m:m + MXU_DIM], avg_ref[...], preferred_element_type=F32)
         + jnp.dot(lo[:, m:m + MXU_DIM], avg_ref[...], preferred_element_type=F32)
         for m in range(0, cb, MXU_DIM)], axis=1)
    o_ref[...] = (v * lax.rsqrt(ms + EPS) * nw_ref[...]).reshape(o_ref.shape)


def _long_conv(g, x0, kh, hy_bias, hy_norm_w, batch, seqlen, kb, cb=256):
    R, C = g.shape
    L = seqlen
    N1 = DFT_N1
    N2 = 2 * L // N1
    K = N1 // 2
    bp = batch // 2
    _, ga, gc, fb, fbc = _mxu_tables(L)
    gv = g.reshape(bp, 2, K, N2, C)
    x0v = x0.reshape(bp, 2, K, N2, C)
    half = pl.BlockSpec((None, 2, K, ROWG, cb), lambda p, j, c: (p, 0, 0, j, c))
    full = pl.BlockSpec((None, 2, N1, ROWG, cb), lambda p, j, c: (p, 0, 0, j, c))
    grid_ac = (bp, N2 // ROWG, C // cb)
    a = pl.pallas_call(
        _fft_a_kernel,
        grid=grid_ac,
        in_specs=[half, pl.BlockSpec((ROWG, 2 * N1, N1), lambda p, j, c: (j, 0, 0))],
        out_specs=full,
        out_shape=jax.ShapeDtypeStruct((bp, 2, N1, N2, C), BF16),
        compiler_params=_params(("arbitrary",) * 3),
        name="fft_a",
    )(gv, ga)
    bt = pl.pallas_call(
        functools.partial(_fft_b_kernel, kb=kb),
        grid=(N1 // kb, bp),
        in_specs=[pl.BlockSpec((None, 2, kb, N2, C), lambda j, p: (p, 0, j, 0, 0)),
                  pl.BlockSpec((2, kb, N2, C), lambda j, p: (0, j, 0, 0)),
                  _const_spec(fb.shape), _const_spec(fbc.shape)],
        out_specs=pl.BlockSpec((None, 2, kb, N2, C), lambda j, p: (p, 0, j, 0, 0)),
        out_shape=jax.ShapeDtypeStruct((bp, 2, N1, N2, C), BF16),
        compiler_params=_params(("arbitrary", "arbitrary")),
        name="fft_b",
    )(a, kh, fb, fbc)
    avg = np.kron(np.eye(MXU_DIM // HY_GROUP), np.full((HY_GROUP, HY_GROUP), 1.0 / HY_GROUP))
    vec = pl.BlockSpec((1, cb), lambda p, j, c: (0, c))
    y = pl.pallas_call(
        _fft_c_kernel,
        grid=grid_ac,
        in_specs=[full, pl.BlockSpec((ROWG, N1, 2 * N1), lambda p, j, c: (j, 0, 0)),
                  half, half, vec, vec, _const_spec((MXU_DIM, MXU_DIM))],
        out_specs=half,
        out_shape=jax.ShapeDtypeStruct((bp, 2, K, N2, C), F32),
        compiler_params=_params(("arbitrary",) * 3),
        name="fft_c",
    )(bt, gc, gv, x0v, hy_bias.reshape(1, C), hy_norm_w.reshape(1, C),
      jnp.asarray(avg.astype(np.float32)).astype(BF16))
    return y.reshape(R, C)


def _mix_kernel(yf_ref, yb_ref, z_ref, yhy_ref, h_ref, p_ref, snw_ref, wos_ref, woh_ref, nmp_ref,
                nfp_ref, wg_ref, wu_ref, wd_ref, nfo_ref, npp_ref, wpg_ref, wpp_ref, npo_ref, o_ref, *, nsub):
    dot = lambda x, w_ref: jnp.dot(x.astype(BF16), w_ref[...], preferred_element_type=F32)
    sub = o_ref.shape[0] // nsub
    tiles = [slice(k * sub, (k + 1) * sub) for k in range(nsub)]
    gw = yf_ref.shape[1] // SSD_GROUPS

    def gated_norm(r):
        ys = (yf_ref[r, :].astype(F32) + yb_ref[r, :].astype(F32)) * _silu(z_ref[r, :].astype(F32))
        return jnp.concatenate(
            [_rms(ys[:, k * gw:(k + 1) * gw], snw_ref[:, k * gw:(k + 1) * gw]) for k in range(SSD_GROUPS)],
            axis=1)

    ysn = [gated_norm(r) for r in tiles]
    mix = [dot(y, wos_ref) + dot(yhy_ref[r, :], woh_ref) for y, r in zip(ysn, tiles)]
    h1 = [h_ref[r, :] + _rms(m, nmp_ref[...]) for m, r in zip(mix, tiles)]
    u = [_rms(x, nfp_ref[...]).astype(BF16) for x in h1]
    act = [(_silu(dot(x, wg_ref)) * dot(x, wu_ref)) for x in u]
    ff = [dot(a, wd_ref) for a in act]
    h2 = [x + _rms(f, nfo_ref[...]) for x, f in zip(h1, ff)]
    gate = [jax.nn.sigmoid(dot(_rms(x, npp_ref[...]), wpg_ref)) for x in h2]
    emb = [dot(p_ref[r, :], wpp_ref) for r in tiles]
    for r, x, gt, e in zip(tiles, h2, gate, emb):
        o_ref[r, :] = x + _rms(gt * e, npo_ref[...])


def _mix(yf, yb, z, yhy, h, p, consts, tm=512, nsub=2):
    R, D = h.shape
    row = lambda w: pl.BlockSpec((tm, w), lambda i: (i, 0))
    return pl.pallas_call(
        functools.partial(_mix_kernel, nsub=nsub),
        grid=(R // tm,),
        in_specs=[row(D), row(D), row(D), row(D), row(D), row(p.shape[1])]
        + [_const_spec(c.shape) for c in consts],
        out_specs=row(D),
        out_shape=jax.ShapeDtypeStruct((R, D), F32),
        compiler_params=_params(("arbitrary",)),
        name="mix",
    )(yf, yb, z, yhy, h, p, *consts)


def _layer(x, p, w):
    batch, seqlen, D = x.shape
    R = batch * seqlen
    x2d = x.reshape(R, D)
    row = lambda v: v.reshape(1, -1)
    o1 = SSD_HEADS * SSD_HEAD_DIM
    o2 = o1 + w["ssd_conv_w"].shape[1]
    o3 = o2 + 2 * SSD_HEADS
    w_in = w["w_in"]
    H = SSD_HEADS
    wdt = jnp.zeros((D, 2 * LANE), F32)
    wdt = wdt.at[:, :H].set(w_in[:, o2:o2 + H]).at[:, LANE:LANE + H].set(w_in[:, o2 + H:o3])
    dtb = jnp.zeros((1, 2 * LANE), F32)
    dtb = dtb.at[0, :H].set(w["ssd_dt_bias"][0]).at[0, LANE:LANE + H].set(w["ssd_dt_bias"][1])
    z, dt, xs, bc, g, x0 = _in_proj(
        x2d, row(w["norm_mix_pre"]), w_in[:, :o1].astype(BF16), w_in[:, o1:o2].astype(BF16),
        w_in[:, o3:].astype(BF16), wdt, dtb, w["ssd_conv_w"], w["ssd_conv_b"], w["hy_conv_w"],
        w["hy_conv_b"], batch, seqlen)
    yf, yb = _ssd(xs, bc, dt, w["ssd_a_log"], w["ssd_d"], batch, seqlen)

    n2 = 2 * seqlen // DFT_N1
    kb = max(1, min(DFT_N1, 256 // n2))
    kh = _filter_spectrum(seqlen, w["hy_f_w1"], w["hy_f_b1"], w["hy_f_w2"], w["hy_f_b2"], w["hy_f_w3"],
                          w["hy_f_b3"], w["hy_f_w4"], w["hy_f_freq"])
    yhy = _long_conv(g, x0, kh, w["hy_bias"], w["hy_norm_w"], batch, seqlen, kb)

    w_out = w["w_out"].astype(BF16)
    consts = [row(w["ssd_norm_w"]), w_out[:o1], w_out[o1:], row(w["norm_mix_post"]),
              row(w["norm_ffn_pre"]), w["w_gate"].astype(BF16), w["w_up"].astype(BF16),
              w["w_down"].astype(BF16), row(w["norm_ffn_post"]), row(w["ple_norm_pre"]),
              w["w_ple_gate"].astype(BF16), w["w_ple_proj"].astype(BF16), row(w["ple_norm_post"])]
    out = _mix(yf, yb, z, yhy, x2d, p.reshape(R, -1), consts)
    return out.reshape(batch, seqlen, D)


_WEIGHT_NAMES = (
    "norm_mix_pre", "w_in", "ssd_conv_w", "ssd_conv_b", "ssd_dt_bias", "ssd_a_log", "ssd_d",
    "ssd_norm_w", "hy_conv_w", "hy_conv_b", "hy_f_w1", "hy_f_b1", "hy_f_w2", "hy_f_b2", "hy_f_w3",
    "hy_f_b3", "hy_f_w4", "hy_f_freq", "hy_bias", "hy_norm_w", "w_out", "norm_mix_post",
    "norm_ffn_pre", "w_gate", "w_up", "w_down", "norm_ffn_post", "ple_norm_pre", "w_ple_gate",
    "w_ple_proj", "ple_norm_post")


def _trunk(x, p, weights):
    h = x
    for i in range(p.shape[0]):
        h = _layer(h, p[i], {n: v[i] for n, v in zip(_WEIGHT_NAMES, weights)})
    return h


def kernel(x_prompt, x_sample, p_prompt, p_sample, norm_mix_pre, w_in, ssd_conv_w, ssd_conv_b, ssd_dt_bias, ssd_a_log, ssd_d, ssd_norm_w, hy_conv_w, hy_conv_b, hy_f_w1, hy_f_b1, hy_f_w2, hy_f_b2, hy_f_w3, hy_f_b3, hy_f_w4, hy_f_freq, hy_bias, hy_norm_w, w_out, norm_mix_post, norm_ffn_pre, w_gate, w_up, w_down, norm_ffn_post, ple_norm_pre, w_ple_gate, w_ple_proj, ple_norm_post):
    weights = (norm_mix_pre, w_in, ssd_conv_w, ssd_conv_b, ssd_dt_bias, ssd_a_log, ssd_d, ssd_norm_w,
               hy_conv_w, hy_conv_b, hy_f_w1, hy_f_b1, hy_f_w2, hy_f_b2, hy_f_w3, hy_f_b3, hy_f_w4,
               hy_f_freq, hy_bias, hy_norm_w, w_out, norm_mix_post, norm_ffn_pre, w_gate, w_up,
               w_down, norm_ffn_post, ple_norm_pre, w_ple_gate, w_ple_proj, ple_norm_post)
    return (_trunk(x_prompt, p_prompt, weights), _trunk(x_sample, p_sample, weights))
```

```python
import functools
import math

import numpy as np
import jax
import jax.numpy as jnp
from jax import lax
from jax.experimental import pallas as pl
from jax.experimental.pallas import tpu as pltpu

F32 = jnp.float32
BF16 = jnp.bfloat16
HIGHEST = lax.Precision.HIGHEST

EPS = 1e-6
LANE = 128
SUBLANE = 8
MXU_DIM = 256
VMEM_LIMIT = 56 * 1024 * 1024

SSD_HEAD_DIM = 64
SSD_HEADS = 16
SSD_GROUPS = 2
SSD_STATE = 128
SSD_CHUNK = 128
HY_GROUP = 64
HY_EMB = 33
HY_FAST_DECAY = 0.3
HY_SLOW_DECAY = 1.5
HY_DECAY_TARGET = 1e-2
DFT_N1 = 128
HALO = 16
ROWG = 16


def _params(sem):
    return pltpu.CompilerParams(dimension_semantics=sem, vmem_limit_bytes=VMEM_LIMIT)


def _const_spec(shape):
    nd = len(shape)
    return pl.BlockSpec(shape, lambda *_: (0,) * nd, pipeline_mode=pl.Buffered(1))


def _rms(x, w):
    return x * lax.rsqrt(jnp.mean(x * x, axis=-1, keepdims=True) + EPS) * w


def _silu(x):
    return x * jax.nn.sigmoid(x)


def _softplus(x):
    return jnp.maximum(x, 0.0) + jnp.log1p(jnp.exp(-jnp.abs(x)))


def _split_bf16(x):
    hi = x.astype(BF16)
    return hi, (x - hi.astype(F32)).astype(BF16)


def _dot3(x, w_hi, w_lo):
    x_hi, x_lo = _split_bf16(x)
    dot = lambda a, b: jnp.dot(a, b, preferred_element_type=F32)
    return dot(x_hi, w_hi) + (dot(x_hi, w_lo) + dot(x_lo, w_hi))


def _shifted(xg, o):
    if o == 0:
        return xg
    sub = lax.broadcasted_iota(jnp.int32, (1, SUBLANE, 1), 1)
    if o < 0:
        w = jnp.where(sub >= SUBLANE + o, jnp.concatenate([xg[:1], xg[:-1]], axis=0), xg)
    else:
        w = jnp.where(sub < o, jnp.concatenate([xg[1:], xg[-1:]], axis=0), xg)
    return pltpu.roll(w, (-o) % SUBLANE, axis=1)


def _dwconv(p_ext, cw_ref, cb_ref, cols, width):
    rows, W = p_ext.shape
    xg = p_ext.reshape(rows // SUBLANE, SUBLANE, W)
    acc = cb_ref[:, cols].reshape(1, 1, W)
    for j in range(width):
        acc = acc + cw_ref[j:j + 1, cols].reshape(1, 1, W) * _shifted(xg, j - width // 2)
    hg = HALO // SUBLANE
    return acc[hg:-hg].reshape(rows - 2 * HALO, W)


def _in_proj_kernel(xm_ref, xp_ref, xn_ref, nw_ref, wz_ref, wxbc_ref, why_ref, wdth_ref, wdtl_ref, dtb_ref,
                    scw_ref, scb_ref, hcw_ref, hcb_ref, z_ref, dt_ref, xs_ref, bc_ref, g_ref, x0_ref,
                    *, ntiles, ssd_w, hy_w, cc):
    i = pl.program_id(1)
    tm = xm_ref.shape[0]
    W = xs_ref.shape[1]
    C = g_ref.shape[1]
    xp = jnp.where(i == 0, 0.0, xp_ref[...])
    xn = jnp.where(i == ntiles - 1, 0.0, xn_ref[...])
    u = _rms(jnp.concatenate([xp, xm_ref[...], xn], axis=0), nw_ref[...])
    ub, ul = _split_bf16(u)
    ubm, ulm = ub[HALO:HALO + tm], ul[HALO:HALO + tm]
    dot = lambda a, b: jnp.dot(a, b, preferred_element_type=F32)
    for j in range(0, W, cc):
        z_ref[:, j:j + cc] = dot(ubm, wz_ref[:, j:j + cc]).astype(z_ref.dtype)
    dt_raw = dot(ubm, wdth_ref[...]) + (dot(ubm, wdtl_ref[...]) + dot(ulm, wdth_ref[...]))
    dt_ref[...] = _softplus(dt_raw + dtb_ref[...])
    for c0 in range(0, wxbc_ref.shape[1], cc):
        cols = slice(c0, c0 + cc)
        act = _silu(_dwconv(dot(ub, wxbc_ref[:, cols]), scw_ref, scb_ref, cols, ssd_w))
        if c0 < W:
            xs_ref[:, cols] = act
        else:
            bc_ref[:, c0 - W:c0 - W + cc] = act.astype(BF16)
    for c0 in range(0, C, cc):
        conv = lambda off: _dwconv(dot(ub, why_ref[:, off + c0:off + c0 + cc]), hcw_ref, hcb_ref,
                                   slice(off + c0, off + c0 + cc), hy_w)
        x0_ref[:, c0:c0 + cc] = conv(0)
        g_ref[:, c0:c0 + cc] = conv(2 * C) * conv(C)


def _in_proj(x2d, nw, wz, wxbc, why, wdt, dt_bias, ssd_cw, ssd_cb, hy_cw, hy_cb, batch, seqlen, tm=512, cc=512):
    R, D = x2d.shape
    W = wz.shape[1]
    XW = wxbc.shape[1]
    HW = why.shape[1]
    C = HW // 3
    wdth, wdtl = _split_bf16(wdt)
    pad_w = lambda w: jnp.zeros((SUBLANE, w.shape[1]), F32).at[:w.shape[0]].set(w)
    nt = seqlen // tm
    hb = tm // HALO
    nhalo = R // HALO
    blk = lambda b, i: b * nt + i
    main = lambda w: pl.BlockSpec((tm, w), lambda b, i: (blk(b, i), 0))
    consts = (nw, wz, wxbc, why, wdth, wdtl, dt_bias, pad_w(ssd_cw), ssd_cb.reshape(1, XW),
              pad_w(hy_cw), hy_cb.reshape(1, HW))
    kern = functools.partial(_in_proj_kernel, ntiles=nt, ssd_w=ssd_cw.shape[0], hy_w=hy_cw.shape[0], cc=cc)
    return pl.pallas_call(
        kern,
        grid=(batch, nt),
        in_specs=[main(D),
                  pl.BlockSpec((HALO, D), lambda b, i: (jnp.maximum(blk(b, i) * hb - 1, 0), 0)),
                  pl.BlockSpec((HALO, D), lambda b, i: (jnp.minimum((blk(b, i) + 1) * hb, nhalo - 1), 0))]
        + [_const_spec(a.shape) for a in consts],
        out_specs=[main(W), main(wdt.shape[1]), main(W), main(XW - W), main(C), main(C)],
        out_shape=[jax.ShapeDtypeStruct((R, W), BF16), jax.ShapeDtypeStruct((R, wdt.shape[1]), F32),
                   jax.ShapeDtypeStruct((R, W), F32), jax.ShapeDtypeStruct((R, XW - W), BF16),
                   jax.ShapeDtypeStruct((R, C), F32), jax.ShapeDtypeStruct((R, C), F32)],
        compiler_params=_params(("arbitrary", "arbitrary")),
        name="in_proj",
    )(x2d, x2d, x2d, *consts)


def _ssd_chunks(chunks):
    T = chunks[0][0].shape[0]
    W = SSD_HEADS * SSD_HEAD_DIM
    GN = SSD_GROUPS * SSD_STATE
    GW = W // SSD_GROUPS
    lo_half = lax.broadcasted_iota(jnp.int32, (T, LANE), 1) < SSD_HEAD_DIM
    lo_row = lo_half[0:1]
    col = lambda v, h: jnp.broadcast_to(v[:, h:h + 1], (v.shape[0], LANE))
    heads_per_group = SSD_HEADS // SSD_GROUPS

    pre = []
    for xs_ref, bc_ref, dt_ref, alog_ref, dskip_ref, tri, y_ref, state_ref in chunks:
        dtv = dt_ref[...]
        a = dtv * (-jnp.exp(alog_ref[...]))
        cs = jnp.dot(tri, a, precision=HIGHEST, preferred_element_type=F32)
        tot = jnp.sum(a, axis=0, keepdims=True)
        pre.append((dtv, tri > 0.5, cs, tot, jnp.exp(tot), cs.T))

    for g in range(SSD_GROUPS):
        gl = slice(g * GW, (g + 1) * GW)
        grp = []
        for (xs_ref, bc_ref, _, _, _, _, _, state_ref) in chunks:
            cg = bc_ref[:, GN + g * SSD_STATE:GN + (g + 1) * SSD_STATE]
            bg = bc_ref[:, g * SSD_STATE:(g + 1) * SSD_STATE]
            scores = lax.dot_general(cg, bg, (((1,), (1,)), ((), ())), preferred_element_type=F32)
            st = state_ref[:, gl]
            grp.append((bg, scores, st, jnp.dot(cg, st.astype(BF16), preferred_element_type=F32), [], []))
        for pr in range(heads_per_group // 2):
            h0 = g * heads_per_group + 2 * pr
            ls = slice(h0 * SSD_HEAD_DIM, h0 * SSD_HEAD_DIM + LANE)
            for ch, (dtv, mask, cs, tot, cdec, csT), (bg, scores, st, yoff, xdec, cdec_g) in zip(chunks, pre, grp):
                xs_ref, dskip_ref, y_ref = ch[0], ch[4], ch[6]
                xpair = xs_ref[:, ls]
                bcs = [col(cs, h0), col(cs, h0 + 1)]
                bcs_p = jnp.where(lo_half, bcs[0], bcs[1])
                pidx = jnp.where(lo_half, h0, h0 + 1)
                xdt = xpair * jnp.take_along_axis(dtv, pidx, axis=1)
                tot_p = jnp.where(lo_row, col(tot, h0), col(tot, h0 + 1))
                yp = yoff[:, pr * LANE:(pr + 1) * LANE] * jnp.exp(bcs_p)
                if dskip_ref is not None:
                    yp = yp + xpair * dskip_ref[:, ls]
                for k in range(2):
                    seg = bcs[k] - csT[h0 + k:h0 + k + 1, :]
                    m = (scores * jnp.exp(jnp.where(mask, seg, -1e30))).astype(BF16)
                    xk = jnp.where(lo_half if k == 0 else ~lo_half, xdt, 0.0).astype(BF16)
                    yp = yp + jnp.dot(m, xk, preferred_element_type=F32)
                y_ref[:, ls] = yp.astype(y_ref.dtype)
                xdec.append((xdt * jnp.exp(tot_p - bcs_p)).astype(BF16))
                cdec_g.append(jnp.where(lo_row, col(cdec, h0), col(cdec, h0 + 1)))
        for ch, (bg, scores, st, yoff, xdec, cdec_g) in zip(chunks, grp):
            upd = lax.dot_general(bg, jnp.concatenate(xdec, axis=1), (((0,), (0,)), ((), ())),
                                  preferred_element_type=F32)
            ch[7][:, gl] = st * jnp.concatenate(cdec_g, axis=1) + upd


def _ssd_kernel(xsf_ref, bcf_ref, dtf_ref, xsb_ref, bcb_ref, dtb_ref, alog_ref, dskip_ref, tri_ref,
                yf_ref, yb_ref, state_ref):
    @pl.when(pl.program_id(1) == 0)
    def _():
        state_ref[...] = jnp.zeros_like(state_ref)

    chunks = []
    for k in range(xsf_ref.shape[0]):
        chunks.append((xsf_ref.at[k], bcf_ref.at[k], dtf_ref.at[k], alog_ref.at[0], dskip_ref, tri_ref[0],
                       yf_ref.at[k], state_ref.at[2 * k]))
        chunks.append((xsb_ref.at[k], bcb_ref.at[k], dtb_ref.at[k], alog_ref.at[1], None, tri_ref[1],
                       yb_ref.at[k], state_ref.at[2 * k + 1]))
    _ssd_chunks(chunks)


def _ssd(xs, bc, dt, a_log, ssd_d, batch, seqlen):
    R, W = xs.shape
    nrows = min(batch, 4)
    T = SSD_CHUNK
    c = seqlen // T
    alog = jnp.zeros((2, 1, LANE), F32).at[:, 0, :SSD_HEADS].set(a_log)
    dskip = jnp.repeat(ssd_d, SSD_HEAD_DIM).reshape(1, W)
    lower = np.tril(np.ones((T, T), np.float32))
    tri = jnp.asarray(np.stack([lower, lower.T]))
    rows4 = lambda a: a.reshape(batch // nrows, nrows, seqlen, a.shape[1])
    fwd = lambda w, lane_blk=0: pl.BlockSpec((None, nrows, T, w), lambda b, i: (b, 0, i, lane_blk))
    bwd = lambda w, lane_blk=0: pl.BlockSpec((None, nrows, T, w), lambda b, i: (b, 0, c - 1 - i, lane_blk))
    xs4, bc4, dt4 = rows4(xs), rows4(bc), rows4(dt)
    yf, yb = pl.pallas_call(
        _ssd_kernel,
        grid=(batch // nrows, c),
        in_specs=[fwd(W), fwd(bc.shape[1]), fwd(LANE, 0), bwd(W), bwd(bc.shape[1]), bwd(LANE, 1),
                  _const_spec(alog.shape), _const_spec(dskip.shape), _const_spec(tri.shape)],
        out_specs=[fwd(W), bwd(W)],
        out_shape=[jax.ShapeDtypeStruct(xs4.shape, BF16)] * 2,
        scratch_shapes=[pltpu.VMEM((2 * nrows, SSD_STATE, W), F32)],
        compiler_params=_params(("arbitrary", "arbitrary")),
        name="ssd",
    )(xs4, bc4, dt4, xs4, bc4, dt4, alog, dskip, tri)
    return yf.reshape(R, W), yb.reshape(R, W)


@functools.lru_cache(maxsize=None)
def _dft_tables(seqlen):
    N = 2 * seqlen
    N1 = DFT_N1
    N2 = N // N1
    K = N1 // 2
    k1 = np.arange(N1).reshape(1, N1, 1)
    n1 = np.arange(K).reshape(1, 1, K)
    n2 = np.arange(N2).reshape(N2, 1, 1)
    ang = -2.0 * np.pi * ((n2 * k1 + N2 * n1 * k1) % N) / N
    gr, gi = np.cos(ang), np.sin(ang)
    gk = np.concatenate([gr, gi], axis=1)
    ga = np.concatenate([np.concatenate([gr, -gi], axis=2),
                         np.concatenate([gi, gr], axis=2)], axis=1)
    grt, git = np.swapaxes(gr, 1, 2), np.swapaxes(gi, 1, 2)
    gc = np.concatenate([np.concatenate([grt, git], axis=2),
                         np.concatenate([-git, grt], axis=2)], axis=1)
    k2 = np.arange(N2).reshape(N2, 1)
    m2 = np.arange(N2).reshape(1, N2)
    ang2 = -2.0 * np.pi * ((k2 * m2) % N2) / N2
    fr, fi = np.cos(ang2), np.sin(ang2)
    fb = np.block([[fr, -fi], [fi, fr]])
    fbc = np.block([[fr, fi], [-fi, fr]])
    to = lambda x: x.astype(np.float32)
    return to(gk), to(ga), to(gc), to(fb), to(fbc)


def _mxu_tables(seqlen):
    return tuple(jnp.asarray(t).astype(BF16) for t in _dft_tables(seqlen))


def _stack_bf16(re, im):
    return jnp.concatenate([re, im], axis=0).astype(BF16)


def _rows_major(x):
    return jnp.swapaxes(x, 0, 1)


def _rows_minor(parts):
    return jnp.swapaxes(jnp.stack(parts, axis=0), 0, 1)


def _filt_a_kernel(fb_ref, w1ah_ref, w1al_ref, w1bh_ref, w1bl_ref, b1_ref, w2h_ref, w2l_ref, b2_ref,
                   w3h_ref, w3l_ref, b3_ref, fr_ref, w4fh_ref, w4fl_ref, w4bh_ref, w4bl_ref, dl_ref, gk_ref,
                   o_ref, h_ref, *, seqlen):
    L = seqlen
    N1 = o_ref.shape[3]
    K = N1 // 2
    N2 = 2 * L // N1
    rows = ROWG * K
    half = rows // 2
    ridx = lax.broadcasted_iota(jnp.int32, (rows, 1), 0)
    n = N2 * (ridx % K) + (pl.program_id(0) * ROWG + ridx // K)
    pos = n.astype(F32)
    t = pos / (L - 1)

    @pl.when(pl.program_id(1) == 0)
    def _():
        lane = lax.broadcasted_iota(jnp.int32, (1, LANE), 1)
        bands = (HY_EMB - 1) // 2
        ang = fb_ref[...] * (2.0 * math.pi * pos / L)
        z = jnp.where(lane == 0, t,
                      jnp.where(lane <= bands, jnp.cos(ang),
                                jnp.where(lane <= 2 * bands, -jnp.sin(ang), 0.0)))
        fr = fr_ref[...]
        pre = _dot3(z[:half], w1ah_ref[...], w1al_ref[...]) + _dot3(z[half:], w1bh_ref[...], w1bl_ref[...])
        h = jnp.sin(fr * (pre + b1_ref[...]))
        h = jnp.sin(fr * (_dot3(h, w2h_ref[...], w2l_ref[...]) + b2_ref[...]))
        h_ref[...] = jnp.sin(fr * (_dot3(h, w3h_ref[...], w3l_ref[...]) + b3_ref[...]))

    h = h_ref[...]
    win = jnp.exp(-t * dl_ref[...])

    def filt(wh_ref, wl_ref):
        top = _dot3(h, wh_ref[0], wl_ref[0])
        bot = _dot3(h, wh_ref[1], wl_ref[1])
        return jnp.concatenate([top, bot], axis=0) * win

    hf = filt(w4fh_ref, w4fl_ref)
    hb = jnp.where(n == 0, 0.0, filt(w4bh_ref, w4bl_ref))
    parts = ((hf + hb).astype(BF16), (hf - hb).astype(BF16))
    for q in range(2):
        for r in range(ROWG):
            y = jnp.dot(gk_ref[r], parts[q][r * K:(r + 1) * K], preferred_element_type=F32)
            o_ref[q, 0, r] = y[:N1].astype(BF16)
            o_ref[q, 1, r] = y[N1:].astype(BF16)


def _filt_b_kernel(a_ref, f_ref, o_ref, *, scale):
    N2 = a_ref.shape[2]
    for part in range(2):
        re, im = _rows_major(a_ref[part, 0]), _rows_major(a_ref[part, 1])
        for q in range(ROWG):
            z = jnp.dot(f_ref[part * N2:(part + 1) * N2, :], jnp.concatenate([re[q], im[q]], axis=0),
                        preferred_element_type=F32)
            o_ref[part, q] = (z * scale).astype(o_ref.dtype)


def _filter_spectrum(seqlen, w1, b1, w2, b2, w3, b3, w4, freq, cb=256):
    L = seqlen
    hidden = w1.shape[1]
    C = w4.shape[1] // 2
    N = 2 * L
    N1 = DFT_N1
    N2 = N // N1
    K = N1 // 2
    ncb = C // cb
    gk, _, _, fb, _ = _mxu_tables(seqlen)
    bands = (HY_EMB - 1) // 2
    fbands = np.zeros((1, LANE), np.float64)
    fbv = np.linspace(1e-4, bands - 1, bands)
    fbands[0, 1:1 + bands] = fbv
    fbands[0, 1 + bands:1 + 2 * bands] = fbv
    max_decay = math.log(HY_DECAY_TARGET) / HY_FAST_DECAY
    min_decay = math.log(HY_DECAY_TARGET) / HY_SLOW_DECAY
    dl = np.abs(np.linspace(min_decay, max_decay, C)).reshape(1, C)
    zeros = lambda r, c: jnp.zeros((r, c), F32)
    w1p = zeros(LANE, hidden).at[:HY_EMB].set(w1)
    w1a = jnp.concatenate([w1p, zeros(LANE, hidden)], axis=1)
    w1b = jnp.concatenate([zeros(LANE, hidden), w1p], axis=1)
    bdiag = lambda w: jnp.concatenate([jnp.concatenate([w, zeros(hidden, hidden)], axis=1),
                                       jnp.concatenate([zeros(hidden, hidden), w], axis=1)], axis=0)
    row2 = lambda v: jnp.concatenate([v, v]).reshape(1, 2 * hidden)
    w4pad = jnp.stack([jnp.concatenate([w4, zeros(hidden, 2 * C)], axis=0),
                       jnp.concatenate([zeros(hidden, 2 * C), w4], axis=0)])
    w4h, w4l = _split_bf16(w4pad)
    mlp = (jnp.asarray(fbands, F32), *_split_bf16(w1a), *_split_bf16(w1b), row2(b1), *_split_bf16(bdiag(w2)),
           row2(b2), *_split_bf16(bdiag(w3)), row2(b3), row2(freq))
    w4f = pl.BlockSpec((2, 2 * hidden, cb), lambda j, c: (0, 0, c))
    w4b = pl.BlockSpec((2, 2 * hidden, cb), lambda j, c: (0, 0, ncb + c))
    ka = pl.pallas_call(
        functools.partial(_filt_a_kernel, seqlen=L),
        grid=(N2 // ROWG, ncb),
        in_specs=[_const_spec(a.shape) for a in mlp]
        + [w4f, w4f, w4b, w4b, pl.BlockSpec((1, cb), lambda j, c: (0, c)),
           pl.BlockSpec((ROWG, 2 * N1, K), lambda j, c: (j, 0, 0))],
        out_specs=pl.BlockSpec((2, 2, ROWG, N1, cb), lambda j, c: (0, 0, j, 0, c)),
        out_shape=jax.ShapeDtypeStruct((2, 2, N2, N1, C), BF16),
        scratch_shapes=[pltpu.VMEM((ROWG * K // 2, 2 * hidden), F32)],
        compiler_params=_params(("arbitrary", "arbitrary")),
        name="filt_a",
    )(*mlp, w4h, w4l, w4h, w4l, jnp.asarray(dl, F32), gk)
    return pl.pallas_call(
        functools.partial(_filt_b_kernel, scale=1.0 / N),
        grid=(N1 // ROWG, ncb),
        in_specs=[pl.BlockSpec((2, 2, N2, ROWG, cb), lambda j, c: (0, 0, 0, j, c)), _const_spec(fb.shape)],
        out_specs=pl.BlockSpec((2, ROWG, N2, cb), lambda j, c: (0, j, 0, c)),
        out_shape=jax.ShapeDtypeStruct((2, N1, N2, C), BF16),
        compiler_params=_params(("arbitrary", "arbitrary")),
        name="filt_b",
    )(ka, fb)


def _fft_a_kernel(x_ref, g_ref, o_ref):
    N1 = o_ref.shape[1]
    xt = [_rows_major(x_ref[part].astype(BF16)) for part in range(2)]
    ys = [jnp.dot(g_ref[r], jnp.concatenate([xt[0][r], xt[1][r]], axis=0), preferred_element_type=F32)
          for r in range(ROWG)]
    for part in range(2):
        o_ref[part] = _rows_minor([y[part * N1:(part + 1) * N1].astype(BF16) for y in ys])


def _fft_b_kernel(a_ref, kh_ref, f_ref, fc_ref, o_ref, *, kb):
    N2 = a_ref.shape[2]
    dot = lambda m, x: jnp.dot(m, x, preferred_element_type=F32)
    zs = [dot(f_ref[...], jnp.concatenate([a_ref[0, q], a_ref[1, q]], axis=0)) for q in range(kb)]
    ws = []
    for q, z in enumerate(zs):
        zr, zi = z[:N2], z[N2:]
        kr, ki = kh_ref[0, q].astype(F32), kh_ref[1, q].astype(F32)
        ws.append(_stack_bf16(zr * kr - zi * ki, zr * ki + zi * kr))
    for q, w in enumerate(ws):
        v = dot(fc_ref[...], w)
        o_ref[0, q] = v[:N2].astype(o_ref.dtype)
        o_ref[1, q] = v[N2:].astype(o_ref.dtype)


def _fft_c_kernel(bt_ref, gc_ref, g_ref, x0_ref, bias_ref, nw_ref, avg_ref, o_ref):
    _, K, _, cb = o_ref.shape
    bt = [_rows_major(bt_ref[part]) for part in range(2)]
    ys = [jnp.dot(gc_ref[r], jnp.concatenate([bt[0][r], bt[1][r]], axis=0), preferred_element_type=F32)
          for r in range(ROWG)]
    rows = 2 * K * ROWG
    conv = jnp.stack([_rows_minor([y[s * K:(s + 1) * K] for y in ys]) for s in range(2)], axis=0)
    flat = lambda x: x.reshape(rows, cb)
    v = flat(x0_ref[...]) * (flat(conv) + flat(g_ref[...]) * bias_ref[...])
    sq = (v * v).astype(BF16)
    ms = jnp.concatenate(
        [jnp.dot(sq[:, m:m + MXU_DIM], avg_ref[...], preferred_element_type=F32)
         for m in range(0, cb, MXU_DIM)], axis=1)
    o_ref[...] = (v * lax.rsqrt(ms + EPS) * nw_ref[...]).reshape(o_ref.shape)


def _long_conv(g, x0, kh, hy_bias, hy_norm_w, batch, seqlen, kb, cb=256):
    R, C = g.shape
    L = seqlen
    N1 = DFT_N1
    N2 = 2 * L // N1
    K = N1 // 2
    bp = batch // 2
    _, ga, gc, fb, fbc = _mxu_tables(L)
    gv = g.reshape(bp, 2, K, N2, C)
    x0v = x0.reshape(bp, 2, K, N2, C)
    half = pl.BlockSpec((None, 2, K, ROWG, cb), lambda p, j, c: (p, 0, 0, j, c))
    full = pl.BlockSpec((None, 2, N1, ROWG, cb), lambda p, j, c: (p, 0, 0, j, c))
    grid_ac = (bp, N2 // ROWG, C // cb)
    a = pl.pallas_call(
        _fft_a_kernel,
        grid=grid_ac,
        in_specs=[half, pl.BlockSpec((ROWG, 2 * N1, N1), lambda p, j, c: (j, 0, 0))],
        out_specs=full,
        out_shape=jax.ShapeDtypeStruct((bp, 2, N1, N2, C), BF16),
        compiler_params=_params(("arbitrary",) * 3),
        name="fft_a",
    )(gv, ga)
    bt = pl.pallas_call(
        functools.partial(_fft_b_kernel, kb=kb),
        grid=(N1 // kb, bp),
        in_specs=[pl.BlockSpec((None, 2, kb, N2, C), lambda j, p: (p, 0, j, 0, 0)),
                  pl.BlockSpec((2, kb, N2, C), lambda j, p: (0, j, 0, 0)),
                  _const_spec(fb.shape), _const_spec(fbc.shape)],
        out_specs=pl.BlockSpec((None, 2, kb, N2, C), lambda j, p: (p, 0, j, 0, 0)),
        out_shape=jax.ShapeDtypeStruct((bp, 2, N1, N2, C), BF16),
        compiler_params=_params(("arbitrary", "arbitrary")),
        name="fft_b",
    )(a, kh, fb, fbc)
    avg = np.kron(np.eye(MXU_DIM // HY_GROUP), np.full((HY_GROUP, HY_GROUP), 1.0 / HY_GROUP))
    vec = pl.BlockSpec((1, cb), lambda p, j, c: (0, c))
    y = pl.pallas_call(
        _fft_c_kernel,
        grid=grid_ac,
        in_specs=[full, pl.BlockSpec((ROWG, N1, 2 * N1), lambda p, j, c: (j, 0, 0)),
                  half, half, vec, vec, _const_spec((MXU_DIM, MXU_DIM))],
        out_specs=half,
        out_shape=jax.ShapeDtypeStruct((bp, 2, K, N2, C), F32),
        compiler_params=_params(("arbitrary",) * 3),
        name="fft_c",
    )(bt, gc, gv, x0v, hy_bias.reshape(1, C), hy_norm_w.reshape(1, C),
      jnp.asarray(avg.astype(np.float32)).astype(BF16))
    return y.reshape(R, C)


def _mix_kernel(yf_ref, yb_ref, z_ref, yhy_ref, h_ref, p_ref, snw_ref, wos_ref, woh_ref, nmp_ref,
                nfp_ref, wg_ref, wu_ref, wd_ref, nfo_ref, npp_ref, wpg_ref, wpp_ref, npo_ref, o_ref, *, nsub):
    dot = lambda x, w_ref: jnp.dot(x.astype(BF16), w_ref[...], preferred_element_type=F32)
    sub = o_ref.shape[0] // nsub
    tiles = [slice(k * sub, (k + 1) * sub) for k in range(nsub)]
    gw = yf_ref.shape[1] // SSD_GROUPS

    def gated_norm(r):
        ys = (yf_ref[r, :].astype(F32) + yb_ref[r, :].astype(F32)) * _silu(z_ref[r, :].astype(F32))
        return jnp.concatenate(
            [_rms(ys[:, k * gw:(k + 1) * gw], snw_ref[:, k * gw:(k + 1) * gw]) for k in range(SSD_GROUPS)],
            axis=1)

    ysn = [gated_norm(r) for r in tiles]
    mix = [dot(y, wos_ref) + dot(yhy_ref[r, :], woh_ref) for y, r in zip(ysn, tiles)]
    h1 = [h_ref[r, :] + _rms(m, nmp_ref[...]) for m, r in zip(mix, tiles)]
    u = [_rms(x, nfp_ref[...]).astype(BF16) for x in h1]
    act = [(_silu(dot(x, wg_ref)) * dot(x, wu_ref)) for x in u]
    ff = [dot(a, wd_ref) for a in act]
    h2 = [x + _rms(f, nfo_ref[...]) for x, f in zip(h1, ff)]
    gate = [jax.nn.sigmoid(dot(_rms(x, npp_ref[...]), wpg_ref)) for x in h2]
    emb = [dot(p_ref[r, :], wpp_ref) for r in tiles]
    for r, x, gt, e in zip(tiles, h2, gate, emb):
        o_ref[r, :] = x + _rms(gt * e, npo_ref[...])


def _mix(yf, yb, z, yhy, h, p, consts, tm=512, nsub=2):
    R, D = h.shape
    row = lambda w: pl.BlockSpec((tm, w), lambda i: (i, 0))
    return pl.pallas_call(
        functools.partial(_mix_kernel, nsub=nsub),
        grid=(R // tm,),
        in_specs=[row(D), row(D), row(D), row(D), row(D), row(p.shape[1])]
        + [_const_spec(c.shape) for c in consts],
        out_specs=row(D),
        out_shape=jax.ShapeDtypeStruct((R, D), F32),
        compiler_params=_params(("arbitrary",)),
        name="mix",
    )(yf, yb, z, yhy, h, p, *consts)


def _layer(x, p, w):
    batch, seqlen, D = x.shape
    R = batch * seqlen
    x2d = x.reshape(R, D)
    row = lambda v: v.reshape(1, -1)
    o1 = SSD_HEADS * SSD_HEAD_DIM
    o2 = o1 + w["ssd_conv_w"].shape[1]
    o3 = o2 + 2 * SSD_HEADS
    w_in = w["w_in"]
    H = SSD_HEADS
    wdt = jnp.zeros((D, 2 * LANE), F32)
    wdt = wdt.at[:, :H].set(w_in[:, o2:o2 + H]).at[:, LANE:LANE + H].set(w_in[:, o2 + H:o3])
    dtb = jnp.zeros((1, 2 * LANE), F32)
    dtb = dtb.at[0, :H].set(w["ssd_dt_bias"][0]).at[0, LANE:LANE + H].set(w["ssd_dt_bias"][1])
    z, dt, xs, bc, g, x0 = _in_proj(
        x2d, row(w["norm_mix_pre"]), w_in[:, :o1].astype(BF16), w_in[:, o1:o2].astype(BF16),
        w_in[:, o3:].astype(BF16), wdt, dtb, w["ssd_conv_w"], w["ssd_conv_b"], w["hy_conv_w"],
        w["hy_conv_b"], batch, seqlen)
    yf, yb = _ssd(xs, bc, dt, w["ssd_a_log"], w["ssd_d"], batch, seqlen)

    n2 = 2 * seqlen // DFT_N1
    kb = max(1, min(DFT_N1, 512 // n2))
    kh = _filter_spectrum(seqlen, w["hy_f_w1"], w["hy_f_b1"], w["hy_f_w2"], w["hy_f_b2"], w["hy_f_w3"],
                          w["hy_f_b3"], w["hy_f_w4"], w["hy_f_freq"])
    yhy = _long_conv(g, x0, kh, w["hy_bias"], w["hy_norm_w"], batch, seqlen, kb)

    w_out = w["w_out"].astype(BF16)
    consts = [row(w["ssd_norm_w"]), w_out[:o1], w_out[o1:], row(w["norm_mix_post"]),
              row(w["norm_ffn_pre"]), w["w_gate"].astype(BF16), w["w_up"].astype(BF16),
              w["w_down"].astype(BF16), row(w["norm_ffn_post"]), row(w["ple_norm_pre"]),
              w["w_ple_gate"].astype(BF16), w["w_ple_proj"].astype(BF16), row(w["ple_norm_post"])]
    out = _mix(yf, yb, z, yhy, x2d, p.reshape(R, -1), consts)
    return out.reshape(batch, seqlen, D)


_WEIGHT_NAMES = (
    "norm_mix_pre", "w_in", "ssd_conv_w", "ssd_conv_b", "ssd_dt_bias", "ssd_a_log", "ssd_d",
    "ssd_norm_w", "hy_conv_w", "hy_conv_b", "hy_f_w1", "hy_f_b1", "hy_f_w2", "hy_f_b2", "hy_f_w3",
    "hy_f_b3", "hy_f_w4", "hy_f_freq", "hy_bias", "hy_norm_w", "w_out", "norm_mix_post",
    "norm_ffn_pre", "w_gate", "w_up", "w_down", "norm_ffn_post", "ple_norm_pre", "w_ple_gate",
    "w_ple_proj", "ple_norm_post")


def _trunk(x, p, weights):
    h = x
    for i in range(p.shape[0]):
        h = _layer(h, p[i], {n: v[i] for n, v in zip(_WEIGHT_NAMES, weights)})
    return h


def kernel(x_prompt, x_sample, p_prompt, p_sample, norm_mix_pre, w_in, ssd_conv_w, ssd_conv_b, ssd_dt_bias, ssd_a_log, ssd_d, ssd_norm_w, hy_conv_w, hy_conv_b, hy_f_w1, hy_f_b1, hy_f_w2, hy_f_b2, hy_f_w3, hy_f_b3, hy_f_w4, hy_f_freq, hy_bias, hy_norm_w, w_out, norm_mix_post, norm_ffn_pre, w_gate, w_up, w_down, norm_ffn_post, ple_norm_pre, w_ple_gate, w_ple_proj, ple_norm_post):
    weights = (norm_mix_pre, w_in, ssd_conv_w, ssd_conv_b, ssd_dt_bias, ssd_a_log, ssd_d, ssd_norm_w,
               hy_conv_w, hy_conv_b, hy_f_w1, hy_f_b1, hy_f_w2, hy_f_b2, hy_f_w3, hy_f_b3, hy_f_w4,
               hy_f_freq, hy_bias, hy_norm_w, w_out, norm_mix_post, norm_ffn_pre, w_gate, w_up,
               w_down, norm_ffn_post, ple_norm_pre, w_ple_gate, w_ple_proj, ple_norm_post)
    return (_trunk(x_prompt, p_prompt, weights), _trunk(x_sample, p_sample, weights))
```

```python
import functools
import math

import numpy as np
import jax
import jax.numpy as jnp
from jax import lax
from jax.experimental import pallas as pl
from jax.experimental.pallas import tpu as pltpu

F32 = jnp.float32
BF16 = jnp.bfloat16

EPS = 1e-6
LANE = 128
SUBLANE = 8
MXU_DIM = 256
VMEM_LIMIT = 56 * 1024 * 1024

SSD_HEAD_DIM = 64
SSD_HEADS = 16
SSD_GROUPS = 2
SSD_STATE = 128
SSD_CHUNK = 128
HY_GROUP = 64
HY_EMB = 33
HY_FAST_DECAY = 0.3
HY_SLOW_DECAY = 1.5
HY_DECAY_TARGET = 1e-2
DFT_N1 = 128
HALO = 16
ROWG = 16


def _params(sem):
    return pltpu.CompilerParams(dimension_semantics=sem, vmem_limit_bytes=VMEM_LIMIT)


def _const_spec(shape):
    nd = len(shape)
    return pl.BlockSpec(shape, lambda *_: (0,) * nd, pipeline_mode=pl.Buffered(1))


def _rms(x, w):
    return x * lax.rsqrt(jnp.mean(x * x, axis=-1, keepdims=True) + EPS) * w


def _silu(x):
    return x * jax.nn.sigmoid(x)


def _softplus(x):
    return jnp.maximum(x, 0.0) + jnp.log1p(jnp.exp(-jnp.abs(x)))


def _split_bf16(x):
    hi = x.astype(BF16)
    return hi, (x - hi.astype(F32)).astype(BF16)


def _dot3(x, w_hi, w_lo):
    x_hi, x_lo = _split_bf16(x)
    dot = lambda a, b: jnp.dot(a, b, preferred_element_type=F32)
    return dot(x_hi, w_hi) + (dot(x_hi, w_lo) + dot(x_lo, w_hi))


def _shifted(xg, o):
    if o == 0:
        return xg
    sub = lax.broadcasted_iota(jnp.int32, (1, SUBLANE, 1), 1)
    if o < 0:
        w = jnp.where(sub >= SUBLANE + o, jnp.concatenate([xg[:1], xg[:-1]], axis=0), xg)
    else:
        w = jnp.where(sub < o, jnp.concatenate([xg[1:], xg[-1:]], axis=0), xg)
    return pltpu.roll(w, (-o) % SUBLANE, axis=1)


def _dwconv(p_ext, cw_ref, cb_ref, cols, width):
    rows, W = p_ext.shape
    xg = p_ext.reshape(rows // SUBLANE, SUBLANE, W)
    acc = cb_ref[:, cols].reshape(1, 1, W)
    for j in range(width):
        acc = acc + cw_ref[j:j + 1, cols].reshape(1, 1, W) * _shifted(xg, j - width // 2)
    hg = HALO // SUBLANE
    return acc[hg:-hg].reshape(rows - 2 * HALO, W)


def _in_proj_kernel(xm_ref, xp_ref, xn_ref, nw_ref, wz_ref, wxbc_ref, why_ref, wdth_ref, wdtl_ref, dtb_ref,
                    scw_ref, scb_ref, hcw_ref, hcb_ref, z_ref, dt_ref, xs_ref, bc_ref, g_ref, x0_ref,
                    *, ntiles, ssd_w, hy_w, cc):
    i = pl.program_id(1)
    tm = xm_ref.shape[0]
    W = xs_ref.shape[1]
    C = g_ref.shape[1]
    xp = jnp.where(i == 0, 0.0, xp_ref[...])
    xn = jnp.where(i == ntiles - 1, 0.0, xn_ref[...])
    u = _rms(jnp.concatenate([xp, xm_ref[...], xn], axis=0), nw_ref[...])
    ub, ul = _split_bf16(u)
    ubm, ulm = ub[HALO:HALO + tm], ul[HALO:HALO + tm]
    dot = lambda a, b: jnp.dot(a, b, preferred_element_type=F32)
    for j in range(0, W, cc):
        z_ref[:, j:j + cc] = dot(ubm, wz_ref[:, j:j + cc]).astype(z_ref.dtype)
    dt_raw = dot(ubm, wdth_ref[...]) + (dot(ubm, wdtl_ref[...]) + dot(ulm, wdth_ref[...]))
    dt_ref[...] = _softplus(dt_raw + dtb_ref[...])
    for c0 in range(0, wxbc_ref.shape[1], cc):
        cols = slice(c0, c0 + cc)
        act = _silu(_dwconv(dot(ub, wxbc_ref[:, cols]), scw_ref, scb_ref, cols, ssd_w))
        if c0 < W:
            xs_ref[:, cols] = act
        else:
            bc_ref[:, c0 - W:c0 - W + cc] = act.astype(BF16)
    for c0 in range(0, C, cc):
        conv = lambda off: _dwconv(dot(ub, why_ref[:, off + c0:off + c0 + cc]), hcw_ref, hcb_ref,
                                   slice(off + c0, off + c0 + cc), hy_w)
        x0_ref[:, c0:c0 + cc] = conv(0)
        g_ref[:, c0:c0 + cc] = conv(2 * C) * conv(C)


def _in_proj(x2d, nw, wz, wxbc, why, wdt, dt_bias, ssd_cw, ssd_cb, hy_cw, hy_cb, batch, seqlen, tm=512, cc=512):
    R, D = x2d.shape
    W = wz.shape[1]
    XW = wxbc.shape[1]
    HW = why.shape[1]
    C = HW // 3
    wdth, wdtl = _split_bf16(wdt)
    pad_w = lambda w: jnp.zeros((SUBLANE, w.shape[1]), F32).at[:w.shape[0]].set(w)
    nt = seqlen // tm
    hb = tm // HALO
    nhalo = R // HALO
    blk = lambda b, i: b * nt + i
    main = lambda w: pl.BlockSpec((tm, w), lambda b, i: (blk(b, i), 0))
    consts = (nw, wz, wxbc, why, wdth, wdtl, dt_bias, pad_w(ssd_cw), ssd_cb.reshape(1, XW),
              pad_w(hy_cw), hy_cb.reshape(1, HW))
    kern = functools.partial(_in_proj_kernel, ntiles=nt, ssd_w=ssd_cw.shape[0], hy_w=hy_cw.shape[0], cc=cc)
    return pl.pallas_call(
        kern,
        grid=(batch, nt),
        in_specs=[main(D),
                  pl.BlockSpec((HALO, D), lambda b, i: (jnp.maximum(blk(b, i) * hb - 1, 0), 0)),
                  pl.BlockSpec((HALO, D), lambda b, i: (jnp.minimum((blk(b, i) + 1) * hb, nhalo - 1), 0))]
        + [_const_spec(a.shape) for a in consts],
        out_specs=[main(W), main(wdt.shape[1]), main(W), main(XW - W), main(C), main(C)],
        out_shape=[jax.ShapeDtypeStruct((R, W), BF16), jax.ShapeDtypeStruct((R, wdt.shape[1]), F32),
                   jax.ShapeDtypeStruct((R, W), F32), jax.ShapeDtypeStruct((R, XW - W), BF16),
                   jax.ShapeDtypeStruct((R, C), F32), jax.ShapeDtypeStruct((R, C), F32)],
        compiler_params=_params(("arbitrary", "arbitrary")),
        name="in_proj",
    )(x2d, x2d, x2d, *consts)


def _cumsum3(tri_b, a):
    dot = lambda x: jnp.dot(tri_b, x, preferred_element_type=F32)
    hi = a.astype(BF16)
    r1 = a - hi.astype(F32)
    mid = r1.astype(BF16)
    lo = (r1 - mid.astype(F32)).astype(BF16)
    return dot(hi) + (dot(mid) + dot(lo))


def _ssd_chunks(chunks):
    T = chunks[0][0].shape[0]
    W = SSD_HEADS * SSD_HEAD_DIM
    GN = SSD_GROUPS * SSD_STATE
    GW = W // SSD_GROUPS
    lo_half = lax.broadcasted_iota(jnp.int32, (T, LANE), 1) < SSD_HEAD_DIM
    lo_row = lo_half[0:1]
    col = lambda v, h: jnp.broadcast_to(v[:, h:h + 1], (v.shape[0], LANE))
    heads_per_group = SSD_HEADS // SSD_GROUPS

    pre = []
    for xs_ref, bc_ref, dt_ref, alog_ref, dskip_ref, tri, y_ref, state_ref in chunks:
        dtv = dt_ref[...]
        a = dtv * (-jnp.exp(alog_ref[...]))
        cs = _cumsum3(tri.astype(BF16), a)
        tot = jnp.sum(a, axis=0, keepdims=True)
        pre.append((dtv, tri > 0.5, cs, tot, jnp.exp(tot), cs.T))

    for g in range(SSD_GROUPS):
        gl = slice(g * GW, (g + 1) * GW)
        grp = []
        for (xs_ref, bc_ref, _, _, _, _, _, state_ref) in chunks:
            cg = bc_ref[:, GN + g * SSD_STATE:GN + (g + 1) * SSD_STATE]
            bg = bc_ref[:, g * SSD_STATE:(g + 1) * SSD_STATE]
            scores = lax.dot_general(cg, bg, (((1,), (1,)), ((), ())), preferred_element_type=F32)
            st = state_ref[:, gl]
            grp.append((bg, scores, st, jnp.dot(cg, st.astype(BF16), preferred_element_type=F32), [], []))
        for pr in range(heads_per_group // 2):
            h0 = g * heads_per_group + 2 * pr
            ls = slice(h0 * SSD_HEAD_DIM, h0 * SSD_HEAD_DIM + LANE)
            for ch, (dtv, mask, cs, tot, cdec, csT), (bg, scores, st, yoff, xdec, cdec_g) in zip(chunks, pre, grp):
                xs_ref, dskip_ref, y_ref = ch[0], ch[4], ch[6]
                xpair = xs_ref[:, ls]
                bcs = [col(cs, h0), col(cs, h0 + 1)]
                bcs_p = jnp.where(lo_half, bcs[0], bcs[1])
                pidx = jnp.where(lo_half, h0, h0 + 1)
                xdt = xpair * jnp.take_along_axis(dtv, pidx, axis=1)
                tot_p = jnp.where(lo_row, col(tot, h0), col(tot, h0 + 1))
                yp = yoff[:, pr * LANE:(pr + 1) * LANE] * jnp.exp(bcs_p)
                if dskip_ref is not None:
                    yp = yp + xpair * dskip_ref[:, ls]
                for k in range(2):
                    seg = bcs[k] - csT[h0 + k:h0 + k + 1, :]
                    m = (scores * jnp.exp(jnp.where(mask, seg, -1e30))).astype(BF16)
                    xk = jnp.where(lo_half if k == 0 else ~lo_half, xdt, 0.0).astype(BF16)
                    yp = yp + jnp.dot(m, xk, preferred_element_type=F32)
                y_ref[:, ls] = yp.astype(y_ref.dtype)
                xdec.append((xdt * jnp.exp(tot_p - bcs_p)).astype(BF16))
                cdec_g.append(jnp.where(lo_row, col(cdec, h0), col(cdec, h0 + 1)))
        for ch, (bg, scores, st, yoff, xdec, cdec_g) in zip(chunks, grp):
            upd = lax.dot_general(bg, jnp.concatenate(xdec, axis=1), (((0,), (0,)), ((), ())),
                                  preferred_element_type=F32)
            ch[7][:, gl] = st * jnp.concatenate(cdec_g, axis=1) + upd


def _ssd_kernel(xsf_ref, bcf_ref, dtf_ref, xsb_ref, bcb_ref, dtb_ref, alog_ref, dskip_ref, tri_ref,
                yf_ref, yb_ref, state_ref):
    @pl.when(pl.program_id(1) == 0)
    def _():
        state_ref[...] = jnp.zeros_like(state_ref)

    chunks = []
    for k in range(xsf_ref.shape[0]):
        chunks.append((xsf_ref.at[k], bcf_ref.at[k], dtf_ref.at[k], alog_ref.at[0], dskip_ref, tri_ref[0],
                       yf_ref.at[k], state_ref.at[2 * k]))
        chunks.append((xsb_ref.at[k], bcb_ref.at[k], dtb_ref.at[k], alog_ref.at[1], None, tri_ref[1],
                       yb_ref.at[k], state_ref.at[2 * k + 1]))
    _ssd_chunks(chunks)


def _ssd(xs, bc, dt, a_log, ssd_d, batch, seqlen):
    R, W = xs.shape
    nrows = min(batch, 4)
    T = SSD_CHUNK
    c = seqlen // T
    alog = jnp.zeros((2, 1, LANE), F32).at[:, 0, :SSD_HEADS].set(a_log)
    dskip = jnp.repeat(ssd_d, SSD_HEAD_DIM).reshape(1, W)
    lower = np.tril(np.ones((T, T), np.float32))
    tri = jnp.asarray(np.stack([lower, lower.T]))
    rows4 = lambda a: a.reshape(batch // nrows, nrows, seqlen, a.shape[1])
    fwd = lambda w, lane_blk=0: pl.BlockSpec((None, nrows, T, w), lambda b, i: (b, 0, i, lane_blk))
    bwd = lambda w, lane_blk=0: pl.BlockSpec((None, nrows, T, w), lambda b, i: (b, 0, c - 1 - i, lane_blk))
    xs4, bc4, dt4 = rows4(xs), rows4(bc), rows4(dt)
    yf, yb = pl.pallas_call(
        _ssd_kernel,
        grid=(batch // nrows, c),
        in_specs=[fwd(W), fwd(bc.shape[1]), fwd(LANE, 0), bwd(W), bwd(bc.shape[1]), bwd(LANE, 1),
                  _const_spec(alog.shape), _const_spec(dskip.shape), _const_spec(tri.shape)],
        out_specs=[fwd(W), bwd(W)],
        out_shape=[jax.ShapeDtypeStruct(xs4.shape, BF16)] * 2,
        scratch_shapes=[pltpu.VMEM((2 * nrows, SSD_STATE, W), F32)],
        compiler_params=_params(("arbitrary", "arbitrary")),
        name="ssd",
    )(xs4, bc4, dt4, xs4, bc4, dt4, alog, dskip, tri)
    return yf.reshape(R, W), yb.reshape(R, W)


@functools.lru_cache(maxsize=None)
def _dft_tables(seqlen):
    N = 2 * seqlen
    N1 = DFT_N1
    N2 = N // N1
    K = N1 // 2
    k1 = np.arange(N1).reshape(1, N1, 1)
    n1 = np.arange(K).reshape(1, 1, K)
    n2 = np.arange(N2).reshape(N2, 1, 1)
    ang = -2.0 * np.pi * ((n2 * k1 + N2 * n1 * k1) % N) / N
    gr, gi = np.cos(ang), np.sin(ang)
    gk = np.concatenate([gr, gi], axis=1)
    ga = np.concatenate([np.concatenate([gr, -gi], axis=2),
                         np.concatenate([gi, gr], axis=2)], axis=1)
    grt, git = np.swapaxes(gr, 1, 2), np.swapaxes(gi, 1, 2)
    gc = np.concatenate([np.concatenate([grt, git], axis=2),
                         np.concatenate([-git, grt], axis=2)], axis=1)
    k2 = np.arange(N2).reshape(N2, 1)
    m2 = np.arange(N2).reshape(1, N2)
    ang2 = -2.0 * np.pi * ((k2 * m2) % N2) / N2
    fr, fi = np.cos(ang2), np.sin(ang2)
    fb = np.block([[fr, -fi], [fi, fr]])
    fbc = np.block([[fr, fi], [-fi, fr]])
    to = lambda x: x.astype(np.float32)
    return to(gk), to(ga), to(gc), to(fb), to(fbc)


def _mxu_tables(seqlen):
    return tuple(jnp.asarray(t).astype(BF16) for t in _dft_tables(seqlen))


def _stack_bf16(re, im):
    return jnp.concatenate([re, im], axis=0).astype(BF16)


def _rows_major(x):
    return jnp.swapaxes(x, 0, 1)


def _rows_minor(parts):
    return jnp.swapaxes(jnp.stack(parts, axis=0), 0, 1)


def _filt_a_kernel(fb_ref, w1ah_ref, w1al_ref, w1bh_ref, w1bl_ref, b1_ref, w2h_ref, w2l_ref, b2_ref,
                   w3h_ref, w3l_ref, b3_ref, fr_ref, w4fh_ref, w4fl_ref, w4bh_ref, w4bl_ref, dl_ref, gk_ref,
                   o_ref, h_ref, *, seqlen):
    L = seqlen
    N1 = o_ref.shape[3]
    K = N1 // 2
    N2 = 2 * L // N1
    rows = ROWG * K
    half = rows // 2
    ridx = lax.broadcasted_iota(jnp.int32, (rows, 1), 0)
    n = N2 * (ridx % K) + (pl.program_id(0) * ROWG + ridx // K)
    pos = n.astype(F32)
    t = pos / (L - 1)

    @pl.when(pl.program_id(1) == 0)
    def _():
        lane = lax.broadcasted_iota(jnp.int32, (1, LANE), 1)
        bands = (HY_EMB - 1) // 2
        ang = fb_ref[...] * (2.0 * math.pi * pos / L)
        z = jnp.where(lane == 0, t,
                      jnp.where(lane <= bands, jnp.cos(ang),
                                jnp.where(lane <= 2 * bands, -jnp.sin(ang), 0.0)))
        fr = fr_ref[...]
        pre = _dot3(z[:half], w1ah_ref[...], w1al_ref[...]) + _dot3(z[half:], w1bh_ref[...], w1bl_ref[...])
        h = jnp.sin(fr * (pre + b1_ref[...]))
        h = jnp.sin(fr * (_dot3(h, w2h_ref[...], w2l_ref[...]) + b2_ref[...]))
        h_ref[...] = jnp.sin(fr * (_dot3(h, w3h_ref[...], w3l_ref[...]) + b3_ref[...]))

    h = h_ref[...]
    win = jnp.exp(-t * dl_ref[...])

    def filt(wh_ref, wl_ref):
        top = _dot3(h, wh_ref[0], wl_ref[0])
        bot = _dot3(h, wh_ref[1], wl_ref[1])
        return jnp.concatenate([top, bot], axis=0) * win

    hf = filt(w4fh_ref, w4fl_ref)
    hb = jnp.where(n == 0, 0.0, filt(w4bh_ref, w4bl_ref))
    parts = ((hf + hb).astype(BF16), (hf - hb).astype(BF16))
    for q in range(2):
        for r in range(ROWG):
            y = jnp.dot(gk_ref[r], parts[q][r * K:(r + 1) * K], preferred_element_type=F32)
            o_ref[q, 0, r] = y[:N1].astype(BF16)
            o_ref[q, 1, r] = y[N1:].astype(BF16)


def _filt_b_kernel(a_ref, f_ref, o_ref, *, scale):
    N2 = a_ref.shape[2]
    for part in range(2):
        re, im = _rows_major(a_ref[part, 0]), _rows_major(a_ref[part, 1])
        for q in range(ROWG):
            z = jnp.dot(f_ref[part * N2:(part + 1) * N2, :], jnp.concatenate([re[q], im[q]], axis=0),
                        preferred_element_type=F32)
            o_ref[part, q] = (z * scale).astype(o_ref.dtype)


def _filter_spectrum(seqlen, w1, b1, w2, b2, w3, b3, w4, freq, cb=256):
    L = seqlen
    hidden = w1.shape[1]
    C = w4.shape[1] // 2
    N = 2 * L
    N1 = DFT_N1
    N2 = N // N1
    K = N1 // 2
    ncb = C // cb
    gk, _, _, fb, _ = _mxu_tables(seqlen)
    bands = (HY_EMB - 1) // 2
    fbands = np.zeros((1, LANE), np.float64)
    fbv = np.linspace(1e-4, bands - 1, bands)
    fbands[0, 1:1 + bands] = fbv
    fbands[0, 1 + bands:1 + 2 * bands] = fbv
    max_decay = math.log(HY_DECAY_TARGET) / HY_FAST_DECAY
    min_decay = math.log(HY_DECAY_TARGET) / HY_SLOW_DECAY
    dl = np.abs(np.linspace(min_decay, max_decay, C)).reshape(1, C)
    zeros = lambda r, c: jnp.zeros((r, c), F32)
    w1p = zeros(LANE, hidden).at[:HY_EMB].set(w1)
    w1a = jnp.concatenate([w1p, zeros(LANE, hidden)], axis=1)
    w1b = jnp.concatenate([zeros(LANE, hidden), w1p], axis=1)
    bdiag = lambda w: jnp.concatenate([jnp.concatenate([w, zeros(hidden, hidden)], axis=1),
                                       jnp.concatenate([zeros(hidden, hidden), w], axis=1)], axis=0)
    row2 = lambda v: jnp.concatenate([v, v]).reshape(1, 2 * hidden)
    w4pad = jnp.stack([jnp.concatenate([w4, zeros(hidden, 2 * C)], axis=0),
                       jnp.concatenate([zeros(hidden, 2 * C), w4], axis=0)])
    w4h, w4l = _split_bf16(w4pad)
    mlp = (jnp.asarray(fbands, F32), *_split_bf16(w1a), *_split_bf16(w1b), row2(b1), *_split_bf16(bdiag(w2)),
           row2(b2), *_split_bf16(bdiag(w3)), row2(b3), row2(freq))
    w4f = pl.BlockSpec((2, 2 * hidden, cb), lambda j, c: (0, 0, c))
    w4b = pl.BlockSpec((2, 2 * hidden, cb), lambda j, c: (0, 0, ncb + c))
    ka = pl.pallas_call(
        functools.partial(_filt_a_kernel, seqlen=L),
        grid=(N2 // ROWG, ncb),
        in_specs=[_const_spec(a.shape) for a in mlp]
        + [w4f, w4f, w4b, w4b, pl.BlockSpec((1, cb), lambda j, c: (0, c)),
           pl.BlockSpec((ROWG, 2 * N1, K), lambda j, c: (j, 0, 0))],
        out_specs=pl.BlockSpec((2, 2, ROWG, N1, cb), lambda j, c: (0, 0, j, 0, c)),
        out_shape=jax.ShapeDtypeStruct((2, 2, N2, N1, C), BF16),
        scratch_shapes=[pltpu.VMEM((ROWG * K // 2, 2 * hidden), F32)],
        compiler_params=_params(("arbitrary", "arbitrary")),
        name="filt_a",
    )(*mlp, w4h, w4l, w4h, w4l, jnp.asarray(dl, F32), gk)
    return pl.pallas_call(
        functools.partial(_filt_b_kernel, scale=1.0 / N),
        grid=(N1 // ROWG, ncb),
        in_specs=[pl.BlockSpec((2, 2, N2, ROWG, cb), lambda j, c: (0, 0, 0, j, c)), _const_spec(fb.shape)],
        out_specs=pl.BlockSpec((2, ROWG, N2, cb), lambda j, c: (0, j, 0, c)),
        out_shape=jax.ShapeDtypeStruct((2, N1, N2, C), BF16),
        compiler_params=_params(("arbitrary", "arbitrary")),
        name="filt_b",
    )(ka, fb)


def _fft_a_kernel(x_ref, g_ref, o_ref):
    N1 = o_ref.shape[1]
    xt = [_rows_major(x_ref[part].astype(BF16)) for part in range(2)]
    ys = [jnp.dot(g_ref[r], jnp.concatenate([xt[0][r], xt[1][r]], axis=0), preferred_element_type=F32)
          for r in range(ROWG)]
    for part in range(2):
        o_ref[part] = _rows_minor([y[part * N1:(part + 1) * N1].astype(BF16) for y in ys])


def _fft_b_kernel(a_ref, kh_ref, f_ref, fc_ref, o_ref, *, kb):
    N2 = a_ref.shape[2]
    dot = lambda m, x: jnp.dot(m, x, preferred_element_type=F32)
    zs = [dot(f_ref[...], jnp.concatenate([a_ref[0, q], a_ref[1, q]], axis=0)) for q in range(kb)]
    ws = []
    for q, z in enumerate(zs):
        zr, zi = z[:N2], z[N2:]
        kr, ki = kh_ref[0, q].astype(F32), kh_ref[1, q].astype(F32)
        ws.append(_stack_bf16(zr * kr - zi * ki, zr * ki + zi * kr))
    for q, w in enumerate(ws):
        v = dot(fc_ref[...], w)
        o_ref[0, q] = v[:N2].astype(o_ref.dtype)
        o_ref[1, q] = v[N2:].astype(o_ref.dtype)


def _fft_c_kernel(bt_ref, gc_ref, g_ref, x0_ref, bias_ref, nw_ref, avg_ref, o_ref):
    _, K, _, cb = o_ref.shape
    bt = [_rows_major(bt_ref[part]) for part in range(2)]
    ys = [jnp.dot(gc_ref[r], jnp.concatenate([bt[0][r], bt[1][r]], axis=0), preferred_element_type=F32)
          for r in range(ROWG)]
    rows = 2 * K * ROWG
    conv = jnp.stack([_rows_minor([y[s * K:(s + 1) * K] for y in ys]) for s in range(2)], axis=0)
    flat = lambda x: x.reshape(rows, cb)
    v = flat(x0_ref[...]) * (flat(conv) + flat(g_ref[...]) * bias_ref[...])
    sq = (v * v).astype(BF16)
    ms = jnp.concatenate(
        [jnp.dot(sq[:, m:m + MXU_DIM], avg_ref[...], preferred_element_type=F32)
         for m in range(0, cb, MXU_DIM)], axis=1)
    o_ref[...] = (v * lax.rsqrt(ms + EPS) * nw_ref[...]).reshape(o_ref.shape)


def _long_conv(g, x0, kh, hy_bias, hy_norm_w, batch, seqlen, kb, cb=256):
    R, C = g.shape
    L = seqlen
    N1 = DFT_N1
    N2 = 2 * L // N1
    K = N1 // 2
    bp = batch // 2
    _, ga, gc, fb, fbc = _mxu_tables(L)
    gv = g.reshape(bp, 2, K, N2, C)
    x0v = x0.reshape(bp, 2, K, N2, C)
    half = pl.BlockSpec((None, 2, K, ROWG, cb), lambda p, j, c: (p, 0, 0, j, c))
    full = pl.BlockSpec((None, 2, N1, ROWG, cb), lambda p, j, c: (p, 0, 0, j, c))
    grid_ac = (bp, N2 // ROWG, C // cb)
    a = pl.pallas_call(
        _fft_a_kernel,
        grid=grid_ac,
        in_specs=[half, pl.BlockSpec((ROWG, 2 * N1, N1), lambda p, j, c: (j, 0, 0))],
        out_specs=full,
        out_shape=jax.ShapeDtypeStruct((bp, 2, N1, N2, C), BF16),
        compiler_params=_params(("arbitrary",) * 3),
        name="fft_a",
    )(gv, ga)
    bt = pl.pallas_call(
        functools.partial(_fft_b_kernel, kb=kb),
        grid=(N1 // kb, bp),
        in_specs=[pl.BlockSpec((None, 2, kb, N2, C), lambda j, p: (p, 0, j, 0, 0)),
                  pl.BlockSpec((2, kb, N2, C), lambda j, p: (0, j, 0, 0)),
                  _const_spec(fb.shape), _const_spec(fbc.shape)],
        out_specs=pl.BlockSpec((None, 2, kb, N2, C), lambda j, p: (p, 0, j, 0, 0)),
        out_shape=jax.ShapeDtypeStruct((bp, 2, N1, N2, C), BF16),
        compiler_params=_params(("arbitrary", "arbitrary")),
        name="fft_b",
    )(a, kh, fb, fbc)
    avg = np.kron(np.eye(MXU_DIM // HY_GROUP), np.full((HY_GROUP, HY_GROUP), 1.0 / HY_GROUP))
    vec = pl.BlockSpec((1, cb), lambda p, j, c: (0, c))
    y = pl.pallas_call(
        _fft_c_kernel,
        grid=grid_ac,
        in_specs=[full, pl.BlockSpec((ROWG, N1, 2 * N1), lambda p, j, c: (j, 0, 0)),
                  half, half, vec, vec, _const_spec((MXU_DIM, MXU_DIM))],
        out_specs=half,
        out_shape=jax.ShapeDtypeStruct((bp, 2, K, N2, C), F32),
        compiler_params=_params(("arbitrary",) * 3),
        name="fft_c",
    )(bt, gc, gv, x0v, hy_bias.reshape(1, C), hy_norm_w.reshape(1, C),
      jnp.asarray(avg.astype(np.float32)).astype(BF16))
    return y.reshape(R, C)


def _mix_kernel(yf_ref, yb_ref, z_ref, yhy_ref, h_ref, p_ref, snw_ref, wos_ref, woh_ref, nmp_ref,
                nfp_ref, wg_ref, wu_ref, wd_ref, nfo_ref, npp_ref, wpg_ref, wpp_ref, npo_ref, o_ref, *, nsub):
    dot = lambda x, w_ref: jnp.dot(x.astype(BF16), w_ref[...], preferred_element_type=F32)
    sub = o_ref.shape[0] // nsub
    tiles = [slice(k * sub, (k + 1) * sub) for k in range(nsub)]
    gw = yf_ref.shape[1] // SSD_GROUPS

    def gated_norm(r):
        ys = (yf_ref[r, :].astype(F32) + yb_ref[r, :].astype(F32)) * _silu(z_ref[r, :].astype(F32))
        return jnp.concatenate(
            [_rms(ys[:, k * gw:(k + 1) * gw], snw_ref[:, k * gw:(k + 1) * gw]) for k in range(SSD_GROUPS)],
            axis=1)

    ysn = [gated_norm(r) for r in tiles]
    mix = [dot(y, wos_ref) + dot(yhy_ref[r, :], woh_ref) for y, r in zip(ysn, tiles)]
    h1 = [h_ref[r, :] + _rms(m, nmp_ref[...]) for m, r in zip(mix, tiles)]
    u = [_rms(x, nfp_ref[...]).astype(BF16) for x in h1]
    act = [(_silu(dot(x, wg_ref)) * dot(x, wu_ref)) for x in u]
    ff = [dot(a, wd_ref) for a in act]
    h2 = [x + _rms(f, nfo_ref[...]) for x, f in zip(h1, ff)]
    gate = [jax.nn.sigmoid(dot(_rms(x, npp_ref[...]), wpg_ref)) for x in h2]
    emb = [dot(p_ref[r, :], wpp_ref) for r in tiles]
    for r, x, gt, e in zip(tiles, h2, gate, emb):
        o_ref[r, :] = x + _rms(gt * e, npo_ref[...])


def _mix(yf, yb, z, yhy, h, p, consts, tm=512, nsub=2):
    R, D = h.shape
    row = lambda w: pl.BlockSpec((tm, w), lambda i: (i, 0))
    return pl.pallas_call(
        functools.partial(_mix_kernel, nsub=nsub),
        grid=(R // tm,),
        in_specs=[row(D), row(D), row(D), row(D), row(D), row(p.shape[1])]
        + [_const_spec(c.shape) for c in consts],
        out_specs=row(D),
        out_shape=jax.ShapeDtypeStruct((R, D), F32),
        compiler_params=_params(("arbitrary",)),
        name="mix",
    )(yf, yb, z, yhy, h, p, *consts)


def _layer(x, p, w):
    batch, seqlen, D = x.shape
    R = batch * seqlen
    x2d = x.reshape(R, D)
    row = lambda v: v.reshape(1, -1)
    o1 = SSD_HEADS * SSD_HEAD_DIM
    o2 = o1 + w["ssd_conv_w"].shape[1]
    o3 = o2 + 2 * SSD_HEADS
    w_in = w["w_in"]
    H = SSD_HEADS
    wdt = jnp.zeros((D, 2 * LANE), F32)
    wdt = wdt.at[:, :H].set(w_in[:, o2:o2 + H]).at[:, LANE:LANE + H].set(w_in[:, o2 + H:o3])
    dtb = jnp.zeros((1, 2 * LANE), F32)
    dtb = dtb.at[0, :H].set(w["ssd_dt_bias"][0]).at[0, LANE:LANE + H].set(w["ssd_dt_bias"][1])
    z, dt, xs, bc, g, x0 = _in_proj(
        x2d, row(w["norm_mix_pre"]), w_in[:, :o1].astype(BF16), w_in[:, o1:o2].astype(BF16),
        w_in[:, o3:].astype(BF16), wdt, dtb, w["ssd_conv_w"], w["ssd_conv_b"], w["hy_conv_w"],
        w["hy_conv_b"], batch, seqlen)
    yf, yb = _ssd(xs, bc, dt, w["ssd_a_log"], w["ssd_d"], batch, seqlen)

    n2 = 2 * seqlen // DFT_N1
    kb = max(1, min(DFT_N1, 512 // n2))
    kh = _filter_spectrum(seqlen, w["hy_f_w1"], w["hy_f_b1"], w["hy_f_w2"], w["hy_f_b2"], w["hy_f_w3"],
                          w["hy_f_b3"], w["hy_f_w4"], w["hy_f_freq"])
    yhy = _long_conv(g, x0, kh, w["hy_bias"], w["hy_norm_w"], batch, seqlen, kb)

    w_out = w["w_out"].astype(BF16)
    consts = [row(w["ssd_norm_w"]), w_out[:o1], w_out[o1:], row(w["norm_mix_post"]),
              row(w["norm_ffn_pre"]), w["w_gate"].astype(BF16), w["w_up"].astype(BF16),
              w["w_down"].astype(BF16), row(w["norm_ffn_post"]), row(w["ple_norm_pre"]),
              w["w_ple_gate"].astype(BF16), w["w_ple_proj"].astype(BF16), row(w["ple_norm_post"])]
    out = _mix(yf, yb, z, yhy, x2d, p.reshape(R, -1), consts)
    return out.reshape(batch, seqlen, D)


_WEIGHT_NAMES = (
    "norm_mix_pre", "w_in", "ssd_conv_w", "ssd_conv_b", "ssd_dt_bias", "ssd_a_log", "ssd_d",
    "ssd_norm_w", "hy_conv_w", "hy_conv_b", "hy_f_w1", "hy_f_b1", "hy_f_w2", "hy_f_b2", "hy_f_w3",
    "hy_f_b3", "hy_f_w4", "hy_f_freq", "hy_bias", "hy_norm_w", "w_out", "norm_mix_post",
    "norm_ffn_pre", "w_gate", "w_up", "w_down", "norm_ffn_post", "ple_norm_pre", "w_ple_gate",
    "w_ple_proj", "ple_norm_post")


def _trunk(x, p, weights):
    h = x
    for i in range(p.shape[0]):
        h = _layer(h, p[i], {n: v[i] for n, v in zip(_WEIGHT_NAMES, weights)})
    return h


def kernel(x_prompt, x_sample, p_prompt, p_sample, norm_mix_pre, w_in, ssd_conv_w, ssd_conv_b, ssd_dt_bias, ssd_a_log, ssd_d, ssd_norm_w, hy_conv_w, hy_conv_b, hy_f_w1, hy_f_b1, hy_f_w2, hy_f_b2, hy_f_w3, hy_f_b3, hy_f_w4, hy_f_freq, hy_bias, hy_norm_w, w_out, norm_mix_post, norm_ffn_pre, w_gate, w_up, w_down, norm_ffn_post, ple_norm_pre, w_ple_gate, w_ple_proj, ple_norm_post):
    weights = (norm_mix_pre, w_in, ssd_conv_w, ssd_conv_b, ssd_dt_bias, ssd_a_log, ssd_d, ssd_norm_w,
               hy_conv_w, hy_conv_b, hy_f_w1, hy_f_b1, hy_f_w2, hy_f_b2, hy_f_w3, hy_f_b3, hy_f_w4,
               hy_f_freq, hy_bias, hy_norm_w, w_out, norm_mix_post, norm_ffn_pre, w_gate, w_up,
               w_down, norm_ffn_post, ple_norm_pre, w_ple_gate, w_ple_proj, ple_norm_post)
    return (_trunk(x_prompt, p_prompt, weights), _trunk(x_sample, p_sample, weights))
```
